```python
import jax, jax.numpy as jnp
from jax import lax
import numpy as np

D_MODEL = 2048
BATCH = 8
SEQ = 2048
DEPTH = 2
DEC_BATCH = 128
DEC_SEQ = 1
PAST_LEN = 16384
PAGE_SIZE = 128

N_MIXERS = 2
N_MLA = (DEPTH + 1) // 2
N_LRU = DEPTH // 2
N_HEADS = 16
Q_LORA = 768
KV_LORA = 512
NOPE_DIM = 128
ROPE_DIM = 64
QK_DIM = NOPE_DIM + ROPE_DIM
V_DIM = 128
ROPE_BASE = 10000.0
ATTN_SCALE = QK_DIM ** -0.5
Q_BLOCK = 128
D_RNN = D_MODEL
LRU_BLOCKS = 8
LRU_BW = D_RNN // LRU_BLOCKS
CONV_W = 4
LRU_C = 8.0
D_FF = ((8 * D_MODEL // 3 + 255) // 256) * 256
EPS = 1e-6
NEG = -1e30

kernel_name = "hybrid_mla_rglru_decode_step"


def rmsnorm(x, g):
    xf = x.astype(jnp.float32)
    y = xf * lax.rsqrt(jnp.mean(xf * xf, axis=-1, keepdims=True) + EPS)
    return (y * g.astype(jnp.float32)).astype(x.dtype)


def rope_cos_sin(pos):
    half = ROPE_DIM // 2
    inv = ROPE_BASE ** (-jnp.arange(half, dtype=jnp.float32) / half)
    ang = pos.astype(jnp.float32)[:, None] * inv[None, :]
    return jnp.cos(ang), jnp.sin(ang)


def apply_rope(x, cos, sin):
    half = ROPE_DIM // 2
    cos = cos.astype(x.dtype)
    sin = sin.astype(x.dtype)
    x1, x2 = x[..., :half], x[..., half:]
    return jnp.concatenate([x1 * cos - x2 * sin, x1 * sin + x2 * cos], axis=-1)


def mla_queries(h, pos, w_dq, q_norm, w_uq, q_gain):
    cos, sin = rope_cos_sin(pos)
    q = jnp.einsum('btr,rhd->bthd', rmsnorm(h @ w_dq, q_norm), w_uq)
    q = jnp.concatenate([q[..., :NOPE_DIM],
                         apply_rope(q[..., NOPE_DIM:], cos[:, None, :], sin[:, None, :])], axis=-1)
    return rmsnorm(q, q_gain)


def mla_latent(h, pos, w_dkv, kv_norm):
    cos, sin = rope_cos_sin(pos)
    kv = h @ w_dkv
    c = rmsnorm(kv[..., :KV_LORA], kv_norm)
    kr = apply_rope(kv[..., KV_LORA:], cos, sin)
    return c, kr


def causal_block_attention(q, k, v):
    B, S, H, _ = q.shape
    nb = S // Q_BLOCK
    qb = jnp.transpose(q.reshape(B, nb, Q_BLOCK, H, QK_DIM), (1, 0, 2, 3, 4))
    kpos = jnp.arange(S)

    def one_block(args):
        qi, blk = args
        s = jnp.einsum('bqhd,bkhd->bhqk', qi, k).astype(jnp.float32) * ATTN_SCALE
        qpos = blk * Q_BLOCK + jnp.arange(Q_BLOCK)
        s = jnp.where(qpos[:, None] >= kpos[None, :], s, NEG)
        p = jax.nn.softmax(s, axis=-1).astype(v.dtype)
        return jnp.einsum('bhqk,bkhd->bqhd', p, v)

    o = lax.map(one_block, (qb, jnp.arange(nb)))
    return jnp.transpose(o, (1, 0, 2, 3, 4)).reshape(B, S, H, V_DIM)


def mla_prompt(h, pos, w_dq, q_norm, w_uq, w_dkv, kv_norm, w_uk, w_uv, q_gain, k_gain, w_o):
    q = mla_queries(h, pos, w_dq, q_norm, w_uq, q_gain)
    c, kr = mla_latent(h, pos, w_dkv, kv_norm)
    k_nope = jnp.einsum('btc,chd->bthd', c, w_uk)
    k = jnp.concatenate([k_nope, jnp.broadcast_to(kr[:, :, None, :], k_nope.shape[:3] + (ROPE_DIM,))], axis=-1)
    k = rmsnorm(k, k_gain)
    v = jnp.einsum('btc,chd->bthd', c, w_uv)
    o = causal_block_attention(q, k, v)
    return jnp.einsum('bthd,hdm->btm', o, w_o), c, kr


def mla_sample(h, pos, ckv_pool, kr_pool, page_table, w_dq, q_norm, w_uq, w_dkv, kv_norm,
               w_uk, w_uv, q_gain, k_gain, w_o):
    q = mla_queries(h, pos, w_dq, q_norm, w_uq, q_gain)
    c_new, kr_new = mla_latent(h, pos, w_dkv, kv_norm)
    qg = q * k_gain.astype(q.dtype)
    q_lat = jnp.einsum('bqhd,chd->bqhc', qg[..., :NOPE_DIM], w_uk)
    q_rr = qg[..., NOPE_DIM:]

    def scores(c, kr):
        k_nope = jnp.einsum('btc,chd->bthd', c, w_uk).astype(jnp.float32)
        krf = kr.astype(jnp.float32)
        ss = jnp.sum(k_nope * k_nope, axis=-1) + jnp.sum(krf * krf, axis=-1)[..., None]
        inv = lax.rsqrt(ss / QK_DIM + EPS)
        raw = (jnp.einsum('bqhc,btc->bhqt', q_lat, c) +
               jnp.einsum('bqhr,btr->bhqt', q_rr, kr)).astype(jnp.float32)
        return raw * jnp.transpose(inv, (0, 2, 1))[:, :, None, :] * ATTN_SCALE

    nq = h.shape[1]
    s_new = scores(c_new, kr_new)
    causal = jnp.arange(nq)[:, None] >= jnp.arange(nq)[None, :]
    s_new = jnp.where(causal, s_new, NEG)
    m0 = jnp.max(s_new, axis=-1)
    p0 = jnp.exp(s_new - m0[..., None])
    l0 = jnp.sum(p0, axis=-1)
    acc0 = jnp.einsum('bhqt,btc->bhqc', p0, c_new.astype(jnp.float32))

    def page_step(carry, ids):
        m, l, acc = carry
        c = ckv_pool[ids]
        kr = kr_pool[ids]
        s = scores(c, kr)
        m_new = jnp.maximum(m, jnp.max(s, axis=-1))
        corr = jnp.exp(m - m_new)
        p = jnp.exp(s - m_new[..., None])
        l = l * corr + jnp.sum(p, axis=-1)
        acc = acc * corr[..., None] + jnp.einsum('bhqt,btc->bhqc', p, c.astype(jnp.float32))
        return (m_new, l, acc), None

    (m, l, acc), _ = lax.scan(page_step, (m0, l0, acc0), jnp.transpose(page_table))
    o_lat = (acc / l[..., None]).astype(h.dtype)
    o = jnp.einsum('bhqc,chd->bqhd', o_lat, w_uv)
    return jnp.einsum('bqhd,hdm->bqm', o, w_o), c_new, kr_new


def rglru_block(h, conv_state, h_state, w_gate, w_in, conv_w, conv_b, w_a, b_a, w_x, b_x, lam, w_out):
    B, T, _ = h.shape
    gate = jax.nn.gelu(h @ w_gate)
    u = h @ w_in
    upad = jnp.concatenate([conv_state.astype(u.dtype), u], axis=1)
    xc = conv_b + sum(conv_w[k] * upad[:, k:k + T] for k in range(CONV_W))
    new_conv = upad[:, T:]
    xb = xc.reshape(B, T, LRU_BLOCKS, LRU_BW)
    r = jax.nn.sigmoid((jnp.einsum('btnj,njk->btnk', xb, w_a) + b_a).astype(jnp.float32)).reshape(B, T, D_RNN)
    ig = jax.nn.sigmoid((jnp.einsum('btnj,njk->btnk', xb, w_x) + b_x).astype(jnp.float32)).reshape(B, T, D_RNN)
    log_a = -LRU_C * r * jax.nn.softplus(-lam.astype(jnp.float32))
    a = jnp.exp(log_a)
    b = jnp.sqrt(-jnp.expm1(2.0 * log_a)) * (ig * xc.astype(jnp.float32))

    def step(hc, ab):
        a_t, b_t = ab
        hc = a_t * hc + b_t
        return hc, hc

    h_last, hs = lax.scan(step, h_state.astype(jnp.float32),
                          (jnp.transpose(a, (1, 0, 2)), jnp.transpose(b, (1, 0, 2))))
    y = jnp.transpose(hs, (1, 0, 2)).astype(h.dtype) * gate
    return y @ w_out, new_conv, h_last.astype(h_state.dtype)


def swiglu(h, w_gate, w_up, w_down):
    return (jax.nn.silu(h @ w_gate) * (h @ w_up)) @ w_down


def setup_inputs(seed: int = 0) -> dict:
    key = jax.random.key(seed)
    ks = iter(jax.random.split(key, 40))
    f32 = jnp.float32

    def nrm(shape, fan_in):
        return jax.random.normal(next(ks), shape, f32) * fan_in ** -0.5

    def gain(shape):
        return 1.0 + 0.02 * jax.random.normal(next(ks), shape, f32)

    def bias(shape):
        return 0.01 * jax.random.normal(next(ks), shape, f32)

    n_pages = PAST_LEN // PAGE_SIZE
    n_pool = (DEC_BATCH * n_pages * 5) // 4
    x_prompt = jax.random.normal(next(ks), (BATCH, SEQ, D_MODEL), f32)
    x_sample = jax.random.normal(next(ks), (DEC_BATCH, DEC_SEQ, D_MODEL), f32)
    cache_ckv = jax.random.normal(next(ks), (N_MLA, n_pool, PAGE_SIZE, KV_LORA), f32)
    cache_krope = jax.random.normal(next(ks), (N_MLA, n_pool, PAGE_SIZE, ROPE_DIM), f32)
    perm = jax.random.permutation(next(ks), n_pool)
    page_table = perm[: DEC_BATCH * n_pages].reshape(DEC_BATCH, n_pages).astype(jnp.int32)
    state_conv = jax.random.normal(next(ks), (N_LRU, DEC_BATCH, CONV_W - 1, D_RNN), f32)
    state_h = 0.5 * jax.random.normal(next(ks), (N_LRU, DEC_BATCH, D_RNN), f32)
    a0 = jax.random.uniform(next(ks), (N_LRU, D_RNN), f32, 0.9, 0.999)
    s0 = a0 ** (1.0 / LRU_C)
    lam = jnp.log(s0) - jnp.log1p(-s0)
    return {
        "x_prompt": x_prompt,
        "x_sample": x_sample,
        "cache_ckv": cache_ckv,
        "cache_krope": cache_krope,
        "page_table": page_table,
        "state_conv": state_conv,
        "state_h": state_h,
        "norm_mix": gain((DEPTH, D_MODEL)),
        "norm_ffn": gain((DEPTH, D_MODEL)),
        "mla_w_dq": nrm((N_MLA, D_MODEL, Q_LORA), D_MODEL),
        "mla_q_norm": gain((N_MLA, Q_LORA)),
        "mla_w_uq": nrm((N_MLA, Q_LORA, N_HEADS, QK_DIM), Q_LORA),
        "mla_w_dkv": nrm((N_MLA, D_MODEL, KV_LORA + ROPE_DIM), D_MODEL),
        "mla_kv_norm": gain((N_MLA, KV_LORA)),
        "mla_w_uk": nrm((N_MLA, KV_LORA, N_HEADS, NOPE_DIM), KV_LORA),
        "mla_w_uv": nrm((N_MLA, KV_LORA, N_HEADS, V_DIM), KV_LORA),
        "mla_q_gain": gain((N_MLA, QK_DIM)),
        "mla_k_gain": gain((N_MLA, QK_DIM)),
        "mla_w_o": nrm((N_MLA, N_HEADS, V_DIM, D_MODEL), N_HEADS * V_DIM),
        "lru_w_gate": nrm((N_LRU, D_MODEL, D_RNN), D_MODEL),
        "lru_w_in": nrm((N_LRU, D_MODEL, D_RNN), D_MODEL),
        "lru_conv_w": nrm((N_LRU, CONV_W, D_RNN), CONV_W),
        "lru_conv_b": bias((N_LRU, D_RNN)),
        "lru_w_a": nrm((N_LRU, LRU_BLOCKS, LRU_BW, LRU_BW), LRU_BW),
        "lru_b_a": bias((N_LRU, LRU_BLOCKS, LRU_BW)),
        "lru_w_x": nrm((N_LRU, LRU_BLOCKS, LRU_BW, LRU_BW), LRU_BW),
        "lru_b_x": bias((N_LRU, LRU_BLOCKS, LRU_BW)),
        "lru_lambda": lam,
        "lru_w_out": nrm((N_LRU, D_RNN, D_MODEL), D_RNN),
        "ffn_w_gate": nrm((DEPTH, D_MODEL, D_FF), D_MODEL),
        "ffn_w_up": nrm((DEPTH, D_MODEL, D_FF), D_MODEL),
        "ffn_w_down": nrm((DEPTH, D_FF, D_MODEL), D_FF),
    }


def reference(x_prompt, x_sample, cache_ckv, cache_krope, page_table, state_conv, state_h,
              norm_mix, norm_ffn, mla_w_dq, mla_q_norm, mla_w_uq, mla_w_dkv, mla_kv_norm,
              mla_w_uk, mla_w_uv, mla_q_gain, mla_k_gain, mla_w_o, lru_w_gate, lru_w_in,
              lru_conv_w, lru_conv_b, lru_w_a, lru_b_a, lru_w_x, lru_b_x, lru_lambda, lru_w_out,
              ffn_w_gate, ffn_w_up, ffn_w_down):
    pos_p = jnp.arange(SEQ)
    pos_s = PAST_LEN + jnp.arange(x_sample.shape[1])
    xp, xs = x_prompt, x_sample
    ckv_p, kr_p, ckv_s, kr_s = [], [], [], []
    conv_p, h_p, conv_s, h_s = [], [], [], []
    for i in range(DEPTH):
        j = i // N_MIXERS
        hp = rmsnorm(xp, norm_mix[i])
        hs = rmsnorm(xs, norm_mix[i])
        if i % N_MIXERS == 0:
            w = (mla_w_dq[j], mla_q_norm[j], mla_w_uq[j], mla_w_dkv[j], mla_kv_norm[j],
                 mla_w_uk[j], mla_w_uv[j], mla_q_gain[j], mla_k_gain[j], mla_w_o[j])
            yp, c1, r1 = mla_prompt(hp, pos_p, *w)
            ys, c2, r2 = mla_sample(hs, pos_s, cache_ckv[j], cache_krope[j], page_table, *w)
            ckv_p.append(c1); kr_p.append(r1); ckv_s.append(c2); kr_s.append(r2)
        else:
            w = (lru_w_gate[j], lru_w_in[j], lru_conv_w[j], lru_conv_b[j], lru_w_a[j], lru_b_a[j],
                 lru_w_x[j], lru_b_x[j], lru_lambda[j], lru_w_out[j])
            zc = jnp.zeros((xp.shape[0], CONV_W - 1, D_RNN), xp.dtype)
            zh = jnp.zeros((xp.shape[0], D_RNN), xp.dtype)
            yp, cv1, hh1 = rglru_block(hp, zc, zh, *w)
            ys, cv2, hh2 = rglru_block(hs, state_conv[j], state_h[j], *w)
            conv_p.append(cv1); h_p.append(hh1); conv_s.append(cv2); h_s.append(hh2)
        xp = xp + yp
        xs = xs + ys
        xp = xp + swiglu(rmsnorm(xp, norm_ffn[i]), ffn_w_gate[i], ffn_w_up[i], ffn_w_down[i])
        xs = xs + swiglu(rmsnorm(xs, norm_ffn[i]), ffn_w_gate[i], ffn_w_up[i], ffn_w_down[i])
    return (xp, xs, jnp.stack(ckv_p), jnp.stack(kr_p), jnp.stack(ckv_s), jnp.stack(kr_s),
            jnp.stack(conv_p), jnp.stack(h_p), jnp.stack(conv_s), jnp.stack(h_s))
```

```python
import functools

import jax
import jax.numpy as jnp
from jax import lax
from jax.experimental import pallas as pl
from jax.experimental.pallas import tpu as pltpu

F32 = jnp.float32
BF16 = jnp.bfloat16

EPS = 1e-6
NEG = -1e30
ROPE_BASE = 10000.0
LRU_C = 8.0
PAGE = 128
PAGES_PER_STEP = 8
VMEM_LIMIT = 56 * 1024 * 1024
NT = (((1,), (1,)), ((), ()))


def _cp(*sem):
    return pltpu.CompilerParams(dimension_semantics=sem, vmem_limit_bytes=VMEM_LIMIT)


def _blk(dim, pref):
    b = min(dim, pref)
    while dim % b:
        b //= 2
    return b


def _rms(xf, gain):
    ms = jnp.mean(xf * xf, axis=-1, keepdims=True)
    return (xf * lax.rsqrt(ms + EPS)) * gain


def _mm_kernel(*refs, n_w, has_norm, has_res, n_out, epilogue):
    refs = list(refs)
    x_ref = refs.pop(0)
    g_ref = refs.pop(0) if has_norm else None
    w_refs = [refs.pop(0) for _ in range(n_w)]
    res_ref = refs.pop(0) if has_res else None
    o_refs = [refs.pop(0) for _ in range(n_out)]
    if has_norm:
        xn_ref = refs.pop(0)

        @pl.when(pl.program_id(1) == 0)
        def _():
            xn_ref[...] = _rms(x_ref[...], g_ref[...]).astype(BF16)

        xb = xn_ref[...]
    else:
        xb = x_ref[...]
    accs = [jnp.dot(xb, w[...], preferred_element_type=F32) for w in w_refs]
    outs = epilogue(accs, res_ref[...] if has_res else None)
    for o_ref, o in zip(o_refs, outs):
        o_ref[...] = o.astype(o_ref.dtype)


def _matmul(x, ws, *, epilogue, out_dtypes, gain=None, res=None, bm=1024, bn=1024):
    m, k = x.shape
    n = ws[0].shape[1]
    bm, bn = _blk(m, bm), _blk(n, bn)
    in_specs = [pl.BlockSpec((bm, k), lambda i, j: (i, 0))]
    args = [x]
    if gain is not None:
        in_specs.append(pl.BlockSpec((1, k), lambda i, j: (0, 0)))
        args.append(gain.reshape(1, k))
    for w in ws:
        in_specs.append(pl.BlockSpec((k, bn), lambda i, j: (0, j)))
        args.append(w)
    if res is not None:
        in_specs.append(pl.BlockSpec((bm, bn), lambda i, j: (i, j)))
        args.append(res)
    kern = functools.partial(_mm_kernel, n_w=len(ws), has_norm=gain is not None,
                             has_res=res is not None, n_out=len(out_dtypes),
                             epilogue=epilogue)
    outs = pl.pallas_call(
        kern,
        grid=(m // bm, n // bn),
        in_specs=in_specs,
        out_specs=[pl.BlockSpec((bm, bn), lambda i, j: (i, j)) for _ in out_dtypes],
        out_shape=[jax.ShapeDtypeStruct((m, n), dt) for dt in out_dtypes],
        scratch_shapes=[pltpu.VMEM((bm, k), BF16)] if gain is not None else [],
        compiler_params=_cp("parallel", "arbitrary"),
    )(*args)
    return outs


def _ep_id(accs, res):
    return [accs[0]]


def _ep_res(accs, res):
    return [res + accs[0]]


def _ep_swiglu(accs, res):
    g, u = accs
    return [(g * jax.nn.sigmoid(g)) * u]


def _ep_gelu_id(accs, res):
    return [jax.nn.gelu(accs[0]), accs[1]]


def _rope_tables(pos, rope_dim):
    half = rope_dim // 2
    inv = ROPE_BASE ** (-jnp.arange(half, dtype=F32) / half)
    ang = pos.astype(F32)[:, None] * inv[None, :]
    cos, sin = jnp.cos(ang), jnp.sin(ang)
    return jnp.concatenate([cos, cos], -1), jnp.concatenate([-sin, sin], -1)


def _rope(x, cos2, sin2):
    half = x.shape[-1] // 2
    swapped = jnp.concatenate([x[:, half:], x[:, :half]], axis=-1)
    return x * cos2 + swapped * sin2


def _qkv_down_kernel(x_ref, g_ref, w_ref, qn_g_ref, kvn_g_ref, cos_ref, sin_ref,
                     qn_ref, c_ref, cb_ref, kr_ref, *, q_lora, kv_lora):
    hb = _rms(x_ref[...], g_ref[...]).astype(BF16)
    a = jnp.dot(hb, w_ref[...], preferred_element_type=F32)
    qn_ref[...] = _rms(a[:, :q_lora], qn_g_ref[...]).astype(BF16)
    c = _rms(a[:, q_lora:q_lora + kv_lora], kvn_g_ref[...])
    c_ref[...] = c
    cb_ref[...] = c.astype(BF16)
    kr_ref[...] = _rope(a[:, q_lora + kv_lora:], cos_ref[...], sin_ref[...])


def _qkv_down(x, gain, w_cat, q_norm, kv_norm, cos2, sin2, *, q_lora, kv_lora):
    m, d = x.shape
    n = w_cat.shape[1]
    rope = n - q_lora - kv_lora
    bm = _blk(m, 512)
    nt = cos2.shape[0] // bm
    row = lambda i: (i, 0)
    fixed = lambda i: (0, 0)
    tab = (lambda i: (i % nt, 0)) if nt > 0 else fixed
    return pl.pallas_call(
        functools.partial(_qkv_down_kernel, q_lora=q_lora, kv_lora=kv_lora),
        grid=(m // bm,),
        in_specs=[pl.BlockSpec((bm, d), row), pl.BlockSpec((1, d), fixed),
                  pl.BlockSpec((d, n), fixed), pl.BlockSpec((1, q_lora), fixed),
                  pl.BlockSpec((1, kv_lora), fixed),
                  pl.BlockSpec((bm, rope), tab), pl.BlockSpec((bm, rope), tab)],
        out_specs=[pl.BlockSpec((bm, q_lora), row), pl.BlockSpec((bm, kv_lora), row),
                   pl.BlockSpec((bm, kv_lora), row), pl.BlockSpec((bm, rope), row)],
        out_shape=[jax.ShapeDtypeStruct((m, q_lora), BF16),
                   jax.ShapeDtypeStruct((m, kv_lora), F32),
                   jax.ShapeDtypeStruct((m, kv_lora), BF16),
                   jax.ShapeDtypeStruct((m, rope), F32)],
        compiler_params=_cp("parallel"),
    )(x, gain.reshape(1, d), w_cat, q_norm.reshape(1, -1), kv_norm.reshape(1, -1), cos2, sin2)


def _q_head(qn, w, cos2, sin2, gain, nope):
    q = jnp.dot(qn, w, preferred_element_type=F32)
    qr = _rope(q[:, nope:], cos2, sin2)
    qq = jnp.concatenate([q[:, :nope], qr], axis=-1)
    return _rms(qq, gain)


def _q_up_kernel(qn_ref, w_ref, cos_ref, sin_ref, g_ref, q_ref, *, nope, scale):
    q = _q_head(qn_ref[...], w_ref[0], cos_ref[...], sin_ref[...], g_ref[...], nope)
    q_ref[0, 0] = (q * scale).astype(BF16)


def _q_up(qn, w_uq_h, cos2, sin2, q_gain, *, batch, nope, scale):
    m, r = qn.shape
    h, _, qk = w_uq_h.shape
    s = m // batch
    bm = _blk(s, 512)
    ns = s // bm
    return pl.pallas_call(
        functools.partial(_q_up_kernel, nope=nope, scale=scale),
        grid=(m // bm, h),
        in_specs=[pl.BlockSpec((bm, r), lambda i, j: (i, 0)),
                  pl.BlockSpec((1, r, qk), lambda i, j: (j, 0, 0)),
                  pl.BlockSpec((bm, qk - nope), lambda i, j: (i % ns, 0)),
                  pl.BlockSpec((bm, qk - nope), lambda i, j: (i % ns, 0)),
                  pl.BlockSpec((1, qk), lambda i, j: (0, 0))],
        out_specs=pl.BlockSpec((1, 1, bm, qk), lambda i, j: (i // ns, j, i % ns, 0)),
        out_shape=jax.ShapeDtypeStruct((batch, h, s, qk), BF16),
        compiler_params=_cp("parallel", "arbitrary"),
    )(qn, w_uq_h, cos2, sin2, q_gain.reshape(1, qk))


def _kv_up_kernel(cb_ref, kr_ref, wuk_ref, wuv_ref, g_ref, k_ref, v_ref, *, heads, nope):
    cb = cb_ref[...]
    kr = kr_ref[...]
    qk = nope + kr.shape[-1]
    kn = jnp.dot(cb, wuk_ref[...], preferred_element_type=F32)
    vv = jnp.dot(cb, wuv_ref[...], preferred_element_type=F32)
    ssr = jnp.sum(kr * kr, axis=-1, keepdims=True)
    g = g_ref[...]
    vd = vv.shape[-1] // heads
    for i in range(heads):
        kh = kn[:, i * nope:(i + 1) * nope]
        inv = lax.rsqrt((jnp.sum(kh * kh, axis=-1, keepdims=True) + ssr) / qk + EPS)
        k_ref[0, i, :, :nope] = ((kh * inv) * g[:, :nope]).astype(BF16)
        k_ref[0, i, :, nope:] = ((kr * inv) * g[:, nope:]).astype(BF16)
        v_ref[0, i] = vv[:, i * vd:(i + 1) * vd].astype(BF16)


def _kv_up(cb, kr, w_uk, w_uv, k_gain, *, batch, n_heads, nope):
    m, c = cb.shape
    rope = kr.shape[1]
    qk = nope + rope
    vd = w_uv.shape[1] // n_heads
    s = m // batch
    bm = _blk(s, 512)
    ns = s // bm
    hg = 4
    return pl.pallas_call(
        functools.partial(_kv_up_kernel, heads=hg, nope=nope),
        grid=(m // bm, n_heads // hg),
        in_specs=[pl.BlockSpec((bm, c), lambda i, j: (i, 0)),
                  pl.BlockSpec((bm, rope), lambda i, j: (i, 0)),
                  pl.BlockSpec((c, hg * nope), lambda i, j: (0, j)),
                  pl.BlockSpec((c, hg * vd), lambda i, j: (0, j)),
                  pl.BlockSpec((1, qk), lambda i, j: (0, 0))],
        out_specs=[pl.BlockSpec((1, hg, bm, qk), lambda i, j: (i // ns, j, i % ns, 0)),
                   pl.BlockSpec((1, hg, bm, vd), lambda i, j: (i // ns, j, i % ns, 0))],
        out_shape=[jax.ShapeDtypeStruct((batch, n_heads, s, qk), BF16),
                   jax.ShapeDtypeStruct((batch, n_heads, s, vd), BF16)],
        compiler_params=_cp("parallel", "arbitrary"),
    )(cb, kr, w_uk, w_uv, k_gain.reshape(1, qk))


def _flash_kernel(q_ref, k_ref, v_ref, o_ref, m_ref, l_ref, acc_ref, *, blk):
    qi = pl.program_id(2)
    kj = pl.program_id(3)

    @pl.when(kj == 0)
    def _():
        m_ref[...] = jnp.full_like(m_ref, NEG)
        l_ref[...] = jnp.zeros_like(l_ref)
        acc_ref[...] = jnp.zeros_like(acc_ref)

    @pl.when(kj <= qi)
    def _():
        kb = k_ref[0, 0]
        s = lax.dot_general(q_ref[0, 0], kb, NT, preferred_element_type=F32)
        qpos = qi * blk + lax.broadcasted_iota(jnp.int32, s.shape, 0)
        kpos = kj * blk + lax.broadcasted_iota(jnp.int32, s.shape, 1)
        s = jnp.where(qpos >= kpos, s, NEG)
        m_prev = m_ref[...]
        m_new = jnp.maximum(m_prev, jnp.max(s, axis=1, keepdims=True))
        corr = jnp.exp(m_prev - m_new)
        p = jnp.exp(s - m_new)
        l_ref[...] = l_ref[...] * corr + jnp.sum(p, axis=1, keepdims=True)
        acc_ref[...] = acc_ref[...] * corr + jnp.dot(
            p.astype(BF16), v_ref[0, 0], preferred_element_type=F32)
        m_ref[...] = m_new

    @pl.when(kj == pl.num_programs(3) - 1)
    def _():
        o_ref[0] = (acc_ref[...] / l_ref[...]).astype(o_ref.dtype)


def _flash(q, k, v):
    b, h, s, qk = q.shape
    vd = v.shape[-1]
    blk = _blk(s, 512)
    n = s // blk
    kv_map = lambda bi, hi, qi, kj: (bi, hi, jnp.minimum(kj, qi), 0)
    return pl.pallas_call(
        functools.partial(_flash_kernel, blk=blk),
        grid=(b, h, n, n),
        in_specs=[pl.BlockSpec((1, 1, blk, qk), lambda bi, hi, qi, kj: (bi, hi, qi, 0)),
                  pl.BlockSpec((1, 1, blk, qk), kv_map),
                  pl.BlockSpec((1, 1, blk, vd), kv_map)],
        out_specs=pl.BlockSpec((1, blk, vd), lambda bi, hi, qi, kj: (bi, qi, hi)),
        out_shape=jax.ShapeDtypeStruct((b, s, h * vd), BF16),
        scratch_shapes=[pltpu.VMEM((blk, 1), F32), pltpu.VMEM((blk, 1), F32),
                        pltpu.VMEM((blk, vd), F32)],
        compiler_params=_cp("parallel", "parallel", "parallel", "arbitrary"),
    )(q, k, v)


def _dec_q_kernel(qn_ref, wq_ref, cos_ref, sin_ref, qg_ref, kg_ref, wuk_ref, c_ref, kr_ref,
                  qlat_ref, qrr_ref, s_ref, *, nope, scale):
    hd = pl.program_id(0)
    q = _q_head(qn_ref[...], wq_ref[0], cos_ref[...], sin_ref[...], qg_ref[...], nope)
    qg = (q * kg_ref[...]) * scale
    wuk = wuk_ref[...]
    qlat = lax.dot_general(qg[:, :nope].astype(BF16), wuk, NT, preferred_element_type=F32)
    qrr = qg[:, nope:]
    qlat_ref[...] = qlat.astype(BF16)
    qrr_ref[0] = qrr.astype(BF16)
    c = c_ref[...]
    kr = kr_ref[...]
    kn = jnp.dot(c.astype(BF16), wuk, preferred_element_type=F32)
    ss = jnp.sum(kn * kn, axis=-1, keepdims=True) + jnp.sum(kr * kr, axis=-1, keepdims=True)
    inv = lax.rsqrt(ss / (nope + kr.shape[-1]) + EPS)
    raw = (jnp.sum(qlat * c, axis=-1, keepdims=True)
           + jnp.sum(qrr * kr, axis=-1, keepdims=True))
    lane = lax.broadcasted_iota(jnp.int32, s_ref.shape, 1)

    @pl.when(hd == 0)
    def _():
        s_ref[...] = jnp.zeros_like(s_ref)

    s_ref[...] = jnp.where(lane == hd, raw * inv, s_ref[...])


def _dec_q(qn, w_uq_h, cos2, sin2, q_gain, k_gain, w_uk, c_new, kr_new, *, nope, scale):
    b, r = qn.shape
    h, _, qk = w_uq_h.shape
    c = c_new.shape[1]
    rope = qk - nope
    fixed = lambda j: (0, 0)
    return pl.pallas_call(
        functools.partial(_dec_q_kernel, nope=nope, scale=scale),
        grid=(h,),
        in_specs=[pl.BlockSpec((b, r), fixed), pl.BlockSpec((1, r, qk), lambda j: (j, 0, 0)),
                  pl.BlockSpec((b, rope), fixed), pl.BlockSpec((b, rope), fixed),
                  pl.BlockSpec((1, qk), fixed), pl.BlockSpec((1, qk), fixed),
                  pl.BlockSpec((c, nope), lambda j: (0, j)),
                  pl.BlockSpec((b, c), fixed), pl.BlockSpec((b, rope), fixed)],
        out_specs=[pl.BlockSpec((b, c), lambda j: (0, j)),
                   pl.BlockSpec((1, b, rope), lambda j: (j, 0, 0)),
                   pl.BlockSpec((b, 128), fixed)],
        out_shape=[jax.ShapeDtypeStruct((b, h * c), BF16),
                   jax.ShapeDtypeStruct((h, b, rope), BF16),
                   jax.ShapeDtypeStruct((b, 128), F32)],
        compiler_params=_cp("arbitrary"),
    )(qn, w_uq_h, cos2, sin2, q_gain.reshape(1, qk), k_gain.reshape(1, qk), w_uk, c_new, kr_new)


def _dec_attn_kernel(pt_ref, qlat_ref, qrr_ref, snew_ref, cnew_ref, wukt_ref, ckv_hbm, kr_hbm,
                     o_ref, cbuf, krbuf, sem, m_ref, l_ref, acc_ref, *, layer, n_heads, qk_dim):
    b = pl.program_id(0)
    ch = pl.program_id(1)
    nb = pl.num_programs(0)
    nch = pl.num_programs(1)
    step = b * nch + ch
    slot = lax.rem(step, 2)

    def copies(bb, cc, sl):
        out = []
        for i in range(PAGES_PER_STEP):
            pg = pt_ref[bb, cc * PAGES_PER_STEP + i]
            rows = pl.ds(i * PAGE, PAGE)
            out.append(pltpu.make_async_copy(ckv_hbm.at[layer, pg], cbuf.at[sl, rows], sem.at[0, sl]))
            out.append(pltpu.make_async_copy(kr_hbm.at[layer, pg], krbuf.at[sl, rows], sem.at[1, sl]))
        return out

    @pl.when(step == 0)
    def _():
        for cp in copies(0, 0, 0):
            cp.start()

    @pl.when(step + 1 < nb * nch)
    def _():
        wrap = ch + 1 == nch
        for cp in copies(jnp.where(wrap, b + 1, b), jnp.where(wrap, 0, ch + 1), 1 - slot):
            cp.start()

    @pl.when(ch == 0)
    def _():
        m_ref[...] = snew_ref[0]
        l_ref[...] = jnp.ones_like(l_ref)
        acc_ref[...] = jnp.broadcast_to(cnew_ref[0], acc_ref.shape)

    for cp in copies(b, ch, slot):
        cp.wait()

    c = cbuf[slot]
    kr = krbuf[slot]
    t = c.shape[0]
    cb = c.astype(BF16)
    knt = lax.dot_general(wukt_ref[...], cb, NT, preferred_element_type=F32)
    nope = knt.shape[0] // n_heads
    ss = jnp.sum((knt * knt).reshape(n_heads, nope, t), axis=1)
    ssr = lax.dot_general(jnp.ones((8, kr.shape[1]), F32), kr * kr, NT,
                          precision=lax.Precision.HIGHEST,
                          preferred_element_type=F32)[0:1]
    raw = (lax.dot_general(qlat_ref[0], cb, NT, preferred_element_type=F32)
           + lax.dot_general(qrr_ref[0], kr.astype(BF16), NT, preferred_element_type=F32))
    s = raw * lax.rsqrt((ss + ssr) / qk_dim + EPS)
    m_prev = m_ref[...]
    m_new = jnp.maximum(m_prev, jnp.max(s, axis=1, keepdims=True))
    corr = jnp.exp(m_prev - m_new)
    p = jnp.exp(s - m_new)
    l_ref[...] = l_ref[...] * corr + jnp.sum(p, axis=1, keepdims=True)
    acc_ref[...] = acc_ref[...] * corr + jnp.dot(p.astype(BF16), cb, preferred_element_type=F32)
    m_ref[...] = m_new

    @pl.when(ch == nch - 1)
    def _():
        o_ref[0] = acc_ref[...] / l_ref[...]


def _dec_attn(page_table, qlat, qrr, s_new, c_new, w_ukt, cache_ckv, cache_krope, *,
              layer, n_heads, qk_dim):
    b, n_pages = page_table.shape
    c = c_new.shape[-1]
    rope = cache_krope.shape[-1]
    t = PAGES_PER_STEP * PAGE
    assert n_pages % PAGES_PER_STEP == 0 and cache_ckv.shape[2] == PAGE
    per_b = lambda bi, ci, pt: (bi, 0, 0)
    grid_spec = pltpu.PrefetchScalarGridSpec(
        num_scalar_prefetch=1,
        grid=(b, n_pages // PAGES_PER_STEP),
        in_specs=[pl.BlockSpec((1, n_heads, c), per_b),
                  pl.BlockSpec((1, n_heads, rope), per_b),
                  pl.BlockSpec((1, n_heads, 1), per_b),
                  pl.BlockSpec((1, 1, c), per_b),
                  pl.BlockSpec(w_ukt.shape, lambda bi, ci, pt: (0, 0)),
                  pl.BlockSpec(memory_space=pl.ANY),
                  pl.BlockSpec(memory_space=pl.ANY)],
        out_specs=pl.BlockSpec((1, n_heads, c), per_b),
        scratch_shapes=[pltpu.VMEM((2, t, c), F32), pltpu.VMEM((2, t, rope), F32),
                        pltpu.SemaphoreType.DMA((2, 2)),
                        pltpu.VMEM((n_heads, 1), F32), pltpu.VMEM((n_heads, 1), F32),
                        pltpu.VMEM((n_heads, c), F32)])
    return pl.pallas_call(
        functools.partial(_dec_attn_kernel, layer=layer, n_heads=n_heads, qk_dim=qk_dim),
        grid_spec=grid_spec,
        out_shape=jax.ShapeDtypeStruct((b, n_heads, c), F32),
        compiler_params=_cp("arbitrary", "arbitrary"),
    )(page_table, qlat, qrr, s_new, c_new, w_ukt, cache_ckv, cache_krope)


def _dec_ov_kernel(o_ref, w_ref, out_ref):
    out_ref[...] = jnp.dot(o_ref[...].astype(BF16), w_ref[...],
                           preferred_element_type=F32).astype(out_ref.dtype)


def _dec_ov(o_lat2, w_uv, *, n_heads):
    b, hc = o_lat2.shape
    c = hc // n_heads
    vd = w_uv.shape[1] // n_heads
    return pl.pallas_call(
        _dec_ov_kernel,
        grid=(n_heads,),
        in_specs=[pl.BlockSpec((b, c), lambda j: (0, j)), pl.BlockSpec((c, vd), lambda j: (0, j))],
        out_specs=pl.BlockSpec((b, vd), lambda j: (0, j)),
        out_shape=jax.ShapeDtypeStruct((b, n_heads * vd), BF16),
        compiler_params=_cp("parallel"),
    )(o_lat2, w_uv)


def _lru_gates(xc, wa, ba, wx, bx, sp):
    xb = xc.astype(BF16)
    r = jax.nn.sigmoid(jnp.dot(xb, wa, preferred_element_type=F32) + ba)
    ig = jax.nn.sigmoid(jnp.dot(xb, wx, preferred_element_type=F32) + bx)
    log_a = (-LRU_C * r) * sp
    a = jnp.exp(log_a)
    bmul = jnp.sqrt(jnp.tanh(-log_a) * (a * a + 1.0))
    return a, bmul * (ig * xc)


def _softplus(x):
    return jnp.maximum(x, 0.0) + jnp.log1p(jnp.exp(-jnp.abs(x)))


def _lru_scan_kernel(u_ref, gate_ref, cw_ref, cb_ref, wa_ref, ba_ref, wx_ref, bx_ref, lam_ref,
                     y_ref, conv_ref, hlast_ref, halo_ref, hc_ref, *, bw):
    ti = pl.program_id(1)
    tb, ch = u_ref.shape

    @pl.when(ti == 0)
    def _():
        halo_ref[...] = jnp.zeros_like(halo_ref)
        hc_ref[...] = jnp.zeros_like(hc_ref)

    u = u_ref[...]
    ext = jnp.concatenate([halo_ref[...], u], axis=0)
    cw = cw_ref[...]
    xc = cb_ref[...] + (((cw[0:1] * ext[5:5 + tb] + cw[1:2] * ext[6:6 + tb])
                         + cw[2:3] * ext[7:7 + tb]) + cw[3:4] * u)
    halo_ref[...] = u[tb - 8:]
    sp = _softplus(-lam_ref[...])
    row = lax.broadcasted_iota(jnp.int32, (tb, bw), 0) % 8
    for n in range(ch // bw):
        cs = slice(n * bw, (n + 1) * bw)
        a, bv = _lru_gates(xc[:, cs], wa_ref[n], ba_ref[:, cs], wx_ref[n], bx_ref[:, cs],
                           sp[:, cs])
        for sh in (1, 2, 4):
            keep = row >= sh
            bv = jnp.where(keep, a * pltpu.roll(bv, sh, 0) + bv, bv)
            a = jnp.where(keep, a * pltpu.roll(a, sh, 0), a)
        hprev = hc_ref[:, cs]
        hs = []
        for g in range(tb // 8):
            hg = a[g * 8:(g + 1) * 8] * hprev + bv[g * 8:(g + 1) * 8]
            hs.append(hg)
            hprev = hg[7:8]
        hc_ref[:, cs] = hprev
        y_ref[:, cs] = (jnp.concatenate(hs, axis=0) * gate_ref[:, cs]).astype(y_ref.dtype)

    @pl.when(ti == pl.num_programs(1) - 1)
    def _():
        conv_ref[0] = u[tb - 3:]
        hlast_ref[0] = hc_ref[...]


def _lru_scan(u, gate, conv_w, conv_b, w_a, b_a, w_x, b_x, lam, *, batch):
    m, ch = u.shape
    t = m // batch
    tb = _blk(t, 256)
    nt = t // tb
    nblk, bw, _ = w_a.shape
    kw = conv_w.shape[0]
    assert kw == 4 and tb % 8 == 0
    row = lambda bi, ti: (bi * nt + ti, 0)
    fixed2 = lambda bi, ti: (0, 0)
    fixed3 = lambda bi, ti: (0, 0, 0)
    per_b = lambda bi, ti: (bi, 0, 0)
    return pl.pallas_call(
        functools.partial(_lru_scan_kernel, bw=bw),
        grid=(batch, nt),
        in_specs=[pl.BlockSpec((tb, ch), row), pl.BlockSpec((tb, ch), row),
                  pl.BlockSpec((kw, ch), fixed2), pl.BlockSpec((1, ch), fixed2),
                  pl.BlockSpec((nblk, bw, bw), fixed3), pl.BlockSpec((1, ch), fixed2),
                  pl.BlockSpec((nblk, bw, bw), fixed3), pl.BlockSpec((1, ch), fixed2),
                  pl.BlockSpec((1, ch), fixed2)],
        out_specs=[pl.BlockSpec((tb, ch), row), pl.BlockSpec((1, kw - 1, ch), per_b),
                   pl.BlockSpec((1, 1, ch), per_b)],
        out_shape=[jax.ShapeDtypeStruct((m, ch), BF16),
                   jax.ShapeDtypeStruct((batch, kw - 1, ch), F32),
                   jax.ShapeDtypeStruct((batch, 1, ch), F32)],
        scratch_shapes=[pltpu.VMEM((8, ch), F32), pltpu.VMEM((1, ch), F32)],
        compiler_params=_cp("parallel", "arbitrary"),
    )(u, gate, conv_w, conv_b.reshape(1, ch), w_a, b_a.reshape(1, ch), w_x, b_x.reshape(1, ch),
      lam.reshape(1, ch))


def _lru_step_kernel(u_ref, gate_ref, s0_ref, s1_ref, s2_ref, h_ref, cw_ref, cb_ref, wa_ref,
                     ba_ref, wx_ref, bx_ref, lam_ref, y_ref, hn_ref):
    cw = cw_ref[...]
    u = u_ref[...]
    xc = cb_ref[...] + (((cw[0:1] * s0_ref[...] + cw[1:2] * s1_ref[...])
                         + cw[2:3] * s2_ref[...]) + cw[3:4] * u)
    a, bv = _lru_gates(xc, wa_ref[0], ba_ref[...], wx_ref[0], bx_ref[...],
                       _softplus(-lam_ref[...]))
    hn = a * h_ref[...] + bv
    hn_ref[...] = hn
    y_ref[...] = (hn * gate_ref[...]).astype(y_ref.dtype)


def _lru_step(u, gate, conv_state2, h_state, conv_w, conv_b, w_a, b_a, w_x, b_x, lam):
    b, ch = u.shape
    nblk, bw, _ = w_a.shape
    kw = conv_w.shape[0]
    assert kw == 4
    col = lambda j: (0, j)
    vec = pl.BlockSpec((1, bw), col)
    state = [pl.BlockSpec((b, bw), (lambda j, kk=kk: (0, kk * nblk + j))) for kk in range(kw - 1)]
    return pl.pallas_call(
        _lru_step_kernel,
        grid=(nblk,),
        in_specs=[pl.BlockSpec((b, bw), col), pl.BlockSpec((b, bw), col), *state,
                  pl.BlockSpec((b, bw), col), pl.BlockSpec((kw, bw), col), vec,
                  pl.BlockSpec((1, bw, bw), lambda j: (j, 0, 0)), vec,
                  pl.BlockSpec((1, bw, bw), lambda j: (j, 0, 0)), vec, vec],
        out_specs=[pl.BlockSpec((b, bw), col), pl.BlockSpec((b, bw), col)],
        out_shape=[jax.ShapeDtypeStruct((b, ch), BF16), jax.ShapeDtypeStruct((b, ch), F32)],
        compiler_params=_cp("parallel"),
    )(u, gate, conv_state2, conv_state2, conv_state2, h_state, conv_w, conv_b.reshape(1, ch),
      w_a, b_a.reshape(1, ch), w_x, b_x.reshape(1, ch), lam.reshape(1, ch))


def _ffn(x, gain, w_gate, w_up, w_down):
    (h,) = _matmul(x, [w_gate, w_up], gain=gain, epilogue=_ep_swiglu, out_dtypes=[BF16],
                   bm=1024, bn=512)
    (y,) = _matmul(h, [w_down], res=x, epilogue=_ep_res, out_dtypes=[F32], bm=1024, bn=512)
    return y


def kernel(x_prompt, x_sample, cache_ckv, cache_krope, page_table, state_conv, state_h, norm_mix, norm_ffn, mla_w_dq, mla_q_norm, mla_w_uq, mla_w_dkv, mla_kv_norm, mla_w_uk, mla_w_uv, mla_q_gain, mla_k_gain, mla_w_o, lru_w_gate, lru_w_in, lru_conv_w, lru_conv_b, lru_w_a, lru_b_a, lru_w_x, lru_b_x, lru_lambda, lru_w_out, ffn_w_gate, ffn_w_up, ffn_w_down):
    bp, sp, d = x_prompt.shape
    bs, ss, _ = x_sample.shape
    assert ss == 1, "sample path handles one new token per sequence"
    depth = norm_mix.shape[0]
    n_heads, nope = mla_w_uk.shape[2], mla_w_uk.shape[3]
    q_lora, kv_lora = mla_w_dq.shape[2], mla_w_uk.shape[1]
    qk = mla_w_uq.shape[3]
    rope = qk - nope
    vd = mla_w_uv.shape[3]
    past = page_table.shape[1] * cache_ckv.shape[2]
    scale = float(qk) ** -0.5

    xp = x_prompt.reshape(bp * sp, d)
    xs = x_sample.reshape(bs, d)
    cos_p, sin_p = _rope_tables(jnp.arange(sp), rope)
    cos_s, sin_s = _rope_tables(jnp.full((bs,), past), rope)

    ckv_p, kr_p, ckv_s, kr_s = [], [], [], []
    conv_p, h_p, conv_s, h_s = [], [], [], []
    for i in range(depth):
        j = i // 2
        if i % 2 == 0:
            w_down = jnp.concatenate([mla_w_dq[j], mla_w_dkv[j]], axis=1).astype(BF16)
            w_uq_h = jnp.transpose(mla_w_uq[j], (1, 0, 2)).astype(BF16)
            w_uk = mla_w_uk[j].reshape(kv_lora, n_heads * nope).astype(BF16)
            w_uv = mla_w_uv[j].reshape(kv_lora, n_heads * vd).astype(BF16)
            w_o = mla_w_o[j].reshape(n_heads * vd, d).astype(BF16)
            down = functools.partial(_qkv_down, gain=norm_mix[i], w_cat=w_down,
                                     q_norm=mla_q_norm[j], kv_norm=mla_kv_norm[j],
                                     q_lora=q_lora, kv_lora=kv_lora)
            qn, c1, cb1, r1 = down(xp, cos2=cos_p, sin2=sin_p)
            q = _q_up(qn, w_uq_h, cos_p, sin_p, mla_q_gain[j], batch=bp, nope=nope, scale=scale)
            k, v = _kv_up(cb1, r1, w_uk, w_uv, mla_k_gain[j], batch=bp, n_heads=n_heads, nope=nope)
            o = _flash(q, k, v).reshape(bp * sp, n_heads * vd)
            (xp,) = _matmul(o, [w_o], res=xp, epilogue=_ep_res, out_dtypes=[F32])
            qn, c2, _, r2 = down(xs, cos2=cos_s, sin2=sin_s)
            qlat, qrr, s_new = _dec_q(qn, w_uq_h, cos_s, sin_s, mla_q_gain[j], mla_k_gain[j],
                                      w_uk, c2, r2, nope=nope, scale=scale)
            o_lat = _dec_attn(page_table, qlat.reshape(bs, n_heads, kv_lora),
                              jnp.transpose(qrr, (1, 0, 2)),
                              s_new[:, :n_heads].reshape(bs, n_heads, 1),
                              c2.reshape(bs, 1, kv_lora), w_uk.T, cache_ckv, cache_krope,
                              layer=j, n_heads=n_heads, qk_dim=qk)
            o = _dec_ov(o_lat.reshape(bs, n_heads * kv_lora), w_uv, n_heads=n_heads)
            (xs,) = _matmul(o, [w_o], res=xs, epilogue=_ep_res, out_dtypes=[F32])
            ckv_p.append(c1.reshape(bp, sp, kv_lora)); kr_p.append(r1.reshape(bp, sp, rope))
            ckv_s.append(c2.reshape(bs, 1, kv_lora)); kr_s.append(r2.reshape(bs, 1, rope))
        else:
            w_gate = lru_w_gate[j].astype(BF16)
            w_in = lru_w_in[j].astype(BF16)
            w_out = lru_w_out[j].astype(BF16)
            w_a = lru_w_a[j].astype(BF16)
            w_x = lru_w_x[j].astype(BF16)
            rest = (lru_conv_w[j], lru_conv_b[j], w_a, lru_b_a[j].reshape(-1), w_x,
                    lru_b_x[j].reshape(-1), lru_lambda[j])
            gate, u = _matmul(xp, [w_gate, w_in], gain=norm_mix[i], epilogue=_ep_gelu_id,
                              out_dtypes=[F32, F32], bm=1024, bn=512)
            y, cv1, hh1 = _lru_scan(u, gate, *rest, batch=bp)
            (xp,) = _matmul(y, [w_out], res=xp, epilogue=_ep_res, out_dtypes=[F32])
            gate, u = _matmul(xs, [w_gate, w_in], gain=norm_mix[i], epilogue=_ep_gelu_id,
                              out_dtypes=[F32, F32], bm=1024, bn=512)
            y, hh2 = _lru_step(u, gate, state_conv[j].reshape(bs, -1), state_h[j], *rest)
            cv2 = jnp.concatenate([state_conv[j][:, 1:], u[:, None, :]], axis=1)
            (xs,) = _matmul(y, [w_out], res=xs, epilogue=_ep_res, out_dtypes=[F32])
            conv_p.append(cv1); h_p.append(hh1.reshape(bp, -1)); conv_s.append(cv2); h_s.append(hh2)
        wg, wu, wd = (ffn_w_gate[i].astype(BF16), ffn_w_up[i].astype(BF16),
                      ffn_w_down[i].astype(BF16))
        xp = _ffn(xp, norm_ffn[i], wg, wu, wd)
        xs = _ffn(xs, norm_ffn[i], wg, wu, wd)
    return (xp.reshape(bp, sp, d), xs.reshape(bs, ss, d),
            jnp.stack(ckv_p), jnp.stack(kr_p), jnp.stack(ckv_s), jnp.stack(kr_s),
            jnp.stack(conv_p), jnp.stack(h_p), jnp.stack(conv_s), jnp.stack(h_s))
```

```python
import functools

import jax
import jax.numpy as jnp
from jax import lax
from jax.experimental import pallas as pl
from jax.experimental.pallas import tpu as pltpu

F32 = jnp.float32
BF16 = jnp.bfloat16

EPS = 1e-6
NEG = -1e30
ROPE_BASE = 10000.0
LRU_C = 8.0
PAGE = 128
PAGES_PER_STEP = 16
FLASH_BQ = 1024
LOG2E = 1.4426950408889634
VMEM_LIMIT = 56 * 1024 * 1024
NT = (((1,), (1,)), ((), ()))


def _cp(*sem):
    return pltpu.CompilerParams(dimension_semantics=sem, vmem_limit_bytes=VMEM_LIMIT)


def _blk(dim, pref):
    b = min(dim, pref)
    while dim % b:
        b //= 2
    return b


def _rms(xf, gain):
    ms = jnp.mean(xf * xf, axis=-1, keepdims=True)
    return (xf * lax.rsqrt(ms + EPS)) * gain


def _mm_kernel(*refs, n_w, has_norm, has_res, n_out, epilogue):
    refs = list(refs)
    x_ref = refs.pop(0)
    g_ref = refs.pop(0) if has_norm else None
    w_refs = [refs.pop(0) for _ in range(n_w)]
    res_ref = refs.pop(0) if has_res else None
    o_refs = [refs.pop(0) for _ in range(n_out)]
    if has_norm:
        xn_ref = refs.pop(0)

        @pl.when(pl.program_id(1) == 0)
        def _():
            xn_ref[...] = _rms(x_ref[...], g_ref[...]).astype(BF16)

        xb = xn_ref[...]
    else:
        xb = x_ref[...]
    accs = [jnp.dot(xb, w[...], preferred_element_type=F32) for w in w_refs]
    outs = epilogue(accs, res_ref[...] if has_res else None)
    for o_ref, o in zip(o_refs, outs):
        o_ref[...] = o.astype(o_ref.dtype)


def _matmul(x, ws, *, name, epilogue, out_dtypes, gain=None, res=None, bm=1024, bn=1024):
    m, k = x.shape
    n = ws[0].shape[1]
    bm, bn = _blk(m, bm), _blk(n, bn)
    in_specs = [pl.BlockSpec((bm, k), lambda i, j: (i, 0))]
    args = [x]
    if gain is not None:
        in_specs.append(pl.BlockSpec((1, k), lambda i, j: (0, 0)))
        args.append(gain.reshape(1, k))
    for w in ws:
        in_specs.append(pl.BlockSpec((k, bn), lambda i, j: (0, j)))
        args.append(w)
    if res is not None:
        in_specs.append(pl.BlockSpec((bm, bn), lambda i, j: (i, j)))
        args.append(res)
    kern = functools.partial(_mm_kernel, n_w=len(ws), has_norm=gain is not None,
                             has_res=res is not None, n_out=len(out_dtypes),
                             epilogue=epilogue)
    outs = pl.pallas_call(
        kern,
        grid=(m // bm, n // bn),
        in_specs=in_specs,
        out_specs=[pl.BlockSpec((bm, bn), lambda i, j: (i, j)) for _ in out_dtypes],
        out_shape=[jax.ShapeDtypeStruct((m, n), dt) for dt in out_dtypes],
        scratch_shapes=[pltpu.VMEM((bm, k), BF16)] if gain is not None else [],
        compiler_params=_cp("parallel", "arbitrary"),
        name=name,
    )(*args)
    return outs


def _ep_res(accs, res):
    return [res + accs[0]]


def _ep_swiglu(accs, res):
    g, u = accs
    return [(g * jax.nn.sigmoid(g)) * u]


def _ep_gelu_id(accs, res):
    return [jax.nn.gelu(accs[0]), accs[1]]


def _rope_tables(pos, rope_dim):
    half = rope_dim // 2
    inv = ROPE_BASE ** (-jnp.arange(half, dtype=F32) / half)
    ang = pos.astype(F32)[:, None] * inv[None, :]
    cos, sin = jnp.cos(ang), jnp.sin(ang)
    return jnp.concatenate([cos, cos], -1), jnp.concatenate([-sin, sin], -1)


def _rope(x, cos2, sin2):
    half = x.shape[-1] // 2
    swapped = jnp.concatenate([x[:, half:], x[:, :half]], axis=-1)
    return x * cos2 + swapped * sin2


def _swap_halves(x, half):
    lane = lax.broadcasted_iota(jnp.int32, x.shape, 1)
    n = x.shape[1]
    return jnp.where(lane % (2 * half) < half, pltpu.roll(x, n - half, 1), pltpu.roll(x, half, 1))


def _qkv_down_kernel(x_ref, g_ref, w_ref, qn_g_ref, kvn_g_ref, cos_ref, sin_ref,
                     qn_ref, c_ref, cb_ref, kr_ref, *, q_lora, kv_lora):
    hb = _rms(x_ref[...], g_ref[...]).astype(BF16)
    a = jnp.dot(hb, w_ref[...], preferred_element_type=F32)
    qn_ref[...] = _rms(a[:, :q_lora], qn_g_ref[...]).astype(BF16)
    c = _rms(a[:, q_lora:q_lora + kv_lora], kvn_g_ref[...])
    c_ref[...] = c
    cb_ref[...] = c.astype(BF16)
    kr_ref[...] = _rope(a[:, q_lora + kv_lora:], cos_ref[...], sin_ref[...])


def _qkv_down(x, gain, w_cat, q_norm, kv_norm, cos2, sin2, *, q_lora, kv_lora):
    m, d = x.shape
    n = w_cat.shape[1]
    rope = n - q_lora - kv_lora
    bm = _blk(m, 512)
    nt = cos2.shape[0] // bm
    row = lambda i: (i, 0)
    fixed = lambda i: (0, 0)
    tab = lambda i: (i % nt, 0)
    return pl.pallas_call(
        functools.partial(_qkv_down_kernel, q_lora=q_lora, kv_lora=kv_lora),
        grid=(m // bm,),
        in_specs=[pl.BlockSpec((bm, d), row), pl.BlockSpec((1, d), fixed),
                  pl.BlockSpec((d, n), fixed), pl.BlockSpec((1, q_lora), fixed),
                  pl.BlockSpec((1, kv_lora), fixed),
                  pl.BlockSpec((bm, rope), tab), pl.BlockSpec((bm, rope), tab)],
        out_specs=[pl.BlockSpec((bm, q_lora), row), pl.BlockSpec((bm, kv_lora), row),
                   pl.BlockSpec((bm, kv_lora), row), pl.BlockSpec((bm, rope), row)],
        out_shape=[jax.ShapeDtypeStruct((m, q_lora), BF16),
                   jax.ShapeDtypeStruct((m, kv_lora), F32),
                   jax.ShapeDtypeStruct((m, kv_lora), BF16),
                   jax.ShapeDtypeStruct((m, rope), F32)],
        compiler_params=_cp("parallel"),
        name="qkv_down",
    )(x, gain.reshape(1, d), w_cat, q_norm.reshape(1, -1), kv_norm.reshape(1, -1), cos2, sin2)


def _q_up_kernel(qn_ref, w_ref, cos_ref, sin_ref, gn_ref, gr_ref, q_ref, *, heads, nope):
    a = jnp.dot(qn_ref[...], w_ref[...], preferred_element_type=F32)
    hn = heads * nope
    rope = (a.shape[1] - hn) // heads
    qk = nope + rope
    r = a[:, hn:]
    r = r * cos_ref[...] + _swap_halves(r, rope // 2) * sin_ref[...]
    for i in range(heads):
        qn = a[:, i * nope:(i + 1) * nope]
        qr = r[:, i * rope:(i + 1) * rope]
        ss = jnp.sum(qn * qn, axis=-1, keepdims=True) + jnp.sum(qr * qr, axis=-1, keepdims=True)
        inv = lax.rsqrt(ss / qk + EPS)
        q_ref[0, i, :, :nope] = ((qn * inv) * gn_ref[...]).astype(BF16)
        q_ref[0, i, :, nope:] = ((qr * inv) * gr_ref[...]).astype(BF16)


def _q_up(qn, w_uq_split, cos_t, sin_t, q_gain, *, batch, n_heads, nope, scale):
    m, r = qn.shape
    n = w_uq_split.shape[1]
    qk = n // n_heads
    rope = qk - nope
    s = m // batch
    bm = _blk(s, 512)
    ns = s // bm
    fixed = lambda i: (0, 0)
    tab = lambda i: (i % ns, 0)
    return pl.pallas_call(
        functools.partial(_q_up_kernel, heads=n_heads, nope=nope),
        grid=(m // bm,),
        in_specs=[pl.BlockSpec((bm, r), lambda i: (i, 0)), pl.BlockSpec((r, n), fixed),
                  pl.BlockSpec((bm, n_heads * rope), tab), pl.BlockSpec((bm, n_heads * rope), tab),
                  pl.BlockSpec((1, nope), fixed), pl.BlockSpec((1, rope), fixed)],
        out_specs=pl.BlockSpec((1, n_heads, bm, qk), lambda i: (i // ns, 0, i % ns, 0)),
        out_shape=jax.ShapeDtypeStruct((batch, n_heads, s, qk), BF16),
        compiler_params=_cp("parallel"),
        name="q_up",
    )(qn, w_uq_split, cos_t, sin_t, (q_gain[:nope] * scale).reshape(1, nope),
      (q_gain[nope:] * scale).reshape(1, rope))


def _kv_up_kernel(cb_ref, kr_ref, wuk_ref, wuvt_ref, g_ref, k_ref, vt_ref, *, heads, nope):
    cb = cb_ref[...]
    kr = kr_ref[...]
    qk = nope + kr.shape[-1]
    kn = jnp.dot(cb, wuk_ref[...], preferred_element_type=F32)
    vt = lax.dot_general(wuvt_ref[...], cb, NT, preferred_element_type=F32)
    ssr = jnp.sum(kr * kr, axis=-1, keepdims=True)
    g = g_ref[...]
    vd = vt.shape[0] // heads
    for i in range(heads):
        kh = kn[:, i * nope:(i + 1) * nope]
        inv = lax.rsqrt((jnp.sum(kh * kh, axis=-1, keepdims=True) + ssr) / qk + EPS)
        k_ref[0, i, :, :nope] = ((kh * inv) * g[:, :nope]).astype(BF16)
        k_ref[0, i, :, nope:] = ((kr * inv) * g[:, nope:]).astype(BF16)
        vt_ref[0, i] = vt[i * vd:(i + 1) * vd].astype(BF16)


def _kv_up(cb, kr, w_uk, w_uvt, k_gain, *, batch, n_heads, nope):
    m, c = cb.shape
    rope = kr.shape[1]
    qk = nope + rope
    vd = w_uvt.shape[0] // n_heads
    s = m // batch
    bm = _blk(s, 512)
    ns = s // bm
    return pl.pallas_call(
        functools.partial(_kv_up_kernel, heads=n_heads, nope=nope),
        grid=(m // bm,),
        in_specs=[pl.BlockSpec((bm, c), lambda i: (i, 0)),
                  pl.BlockSpec((bm, rope), lambda i: (i, 0)),
                  pl.BlockSpec(w_uk.shape, lambda i: (0, 0)),
                  pl.BlockSpec(w_uvt.shape, lambda i: (0, 0)),
                  pl.BlockSpec((1, qk), lambda i: (0, 0))],
        out_specs=[pl.BlockSpec((1, n_heads, bm, qk), lambda i: (i // ns, 0, i % ns, 0)),
                   pl.BlockSpec((1, n_heads, vd, bm), lambda i: (i // ns, 0, 0, i % ns))],
        out_shape=[jax.ShapeDtypeStruct((batch, n_heads, s, qk), BF16),
                   jax.ShapeDtypeStruct((batch, n_heads, vd, s), BF16)],
        compiler_params=_cp("parallel"),
        name="kv_up",
    )(cb, kr, w_uk, w_uvt, k_gain.reshape(1, qk))


def _flash_kernel(q_ref, k_ref, vt_ref, o_ref, *, bq):
    s_len = q_ref.shape[2]
    kpos = lax.broadcasted_iota(jnp.int32, (bq, bq), 0)
    qpos = lax.broadcasted_iota(jnp.int32, (bq, bq), 1)
    for qi in range(s_len // bq):
        nk = (qi + 1) * bq
        q = q_ref[0, 0, qi * bq:nk, :]
        st = lax.dot_general(k_ref[0, 0, 0:nk, :], q, NT, preferred_element_type=F32)
        diag = jnp.where(kpos <= qpos, st[nk - bq:], NEG)
        st = diag if qi == 0 else jnp.concatenate([st[:nk - bq], diag], axis=0)
        m = jnp.max(st, axis=0, keepdims=True)
        p = jnp.exp2(st - m)
        l = jnp.sum(p, axis=0, keepdims=True)
        acct = jnp.dot(vt_ref[0, 0, :, 0:nk], p.astype(BF16), preferred_element_type=F32)
        o_ref[0, qi * bq:nk, :] = jnp.transpose(acct / l).astype(o_ref.dtype)


def _flash(q, k, vt):
    b, h, s, qk = q.shape
    vd = vt.shape[2]
    bq = _blk(s, FLASH_BQ)
    head = lambda bi, hi: (bi, hi, 0, 0)
    return pl.pallas_call(
        functools.partial(_flash_kernel, bq=bq),
        grid=(b, h),
        in_specs=[pl.BlockSpec((1, 1, s, qk), head), pl.BlockSpec((1, 1, s, qk), head),
                  pl.BlockSpec((1, 1, vd, s), head)],
        out_specs=pl.BlockSpec((1, s, vd), lambda bi, hi: (bi, 0, hi)),
        out_shape=jax.ShapeDtypeStruct((b, s, h * vd), BF16),
        compiler_params=_cp("parallel", "parallel"),
        name="flash",
    )(q, k, vt)


def _dec_q_kernel(qn_ref, wq_ref, cos_ref, sin_ref, qg_ref, kg_ref, wuk_ref, c_ref, kr_ref,
                  qlat_ref, qrr_ref, s_ref, *, nope, scale):
    hd = pl.program_id(0)
    q = jnp.dot(qn_ref[...], wq_ref[0], preferred_element_type=F32)
    q = jnp.concatenate([q[:, :nope], _rope(q[:, nope:], cos_ref[...], sin_ref[...])], axis=-1)
    qg = (_rms(q, qg_ref[...]) * kg_ref[...]) * scale
    wuk = wuk_ref[...]
    qlat = lax.dot_general(qg[:, :nope].astype(BF16), wuk, NT, preferred_element_type=F32)
    qrr = qg[:, nope:]
    qlat_ref[...] = qlat.astype(BF16)
    qrr_ref[0] = qrr.astype(BF16)
    c = c_ref[...]
    kr = kr_ref[...]
    kn = jnp.dot(c.astype(BF16), wuk, preferred_element_type=F32)
    ss = jnp.sum(kn * kn, axis=-1, keepdims=True) + jnp.sum(kr * kr, axis=-1, keepdims=True)
    inv = lax.rsqrt(ss / (nope + kr.shape[-1]) + EPS)
    raw = (jnp.sum(qlat * c, axis=-1, keepdims=True)
           + jnp.sum(qrr * kr, axis=-1, keepdims=True))
    lane = lax.broadcasted_iota(jnp.int32, s_ref.shape, 1)

    @pl.when(hd == 0)
    def _():
        s_ref[...] = jnp.zeros_like(s_ref)

    s_ref[...] = jnp.where(lane == hd, raw * inv, s_ref[...])


def _dec_q(qn, w_uq_h, cos2, sin2, q_gain, k_gain, w_uk, c_new, kr_new, *, nope, scale):
    b, r = qn.shape
    h, _, qk = w_uq_h.shape
    c = c_new.shape[1]
    rope = qk - nope
    fixed = lambda j: (0, 0)
    return pl.pallas_call(
        functools.partial(_dec_q_kernel, nope=nope, scale=scale),
        grid=(h,),
        in_specs=[pl.BlockSpec((b, r), fixed), pl.BlockSpec((1, r, qk), lambda j: (j, 0, 0)),
                  pl.BlockSpec((b, rope), fixed), pl.BlockSpec((b, rope), fixed),
                  pl.BlockSpec((1, qk), fixed), pl.BlockSpec((1, qk), fixed),
                  pl.BlockSpec((c, nope), lambda j: (0, j)),
                  pl.BlockSpec((b, c), fixed), pl.BlockSpec((b, rope), fixed)],
        out_specs=[pl.BlockSpec((b, c), lambda j: (0, j)),
                   pl.BlockSpec((1, b, rope), lambda j: (j, 0, 0)),
                   pl.BlockSpec((b, 128), fixed)],
        out_shape=[jax.ShapeDtypeStruct((b, h * c), BF16),
                   jax.ShapeDtypeStruct((h, b, rope), BF16),
                   jax.ShapeDtypeStruct((b, 128), F32)],
        compiler_params=_cp("arbitrary"),
        name="dec_q",
    )(qn, w_uq_h, cos2, sin2, q_gain.reshape(1, qk), k_gain.reshape(1, qk), w_uk, c_new, kr_new)


def _dec_attn_kernel(pt_ref, qlat_ref, qrr_ref, snew_ref, cnew_ref, wukt_ref, ckv_hbm, krt_hbm,
                     o_ref, cbuf, krbuf, lhs_ref, sem, m_ref, l_ref, acc_ref, *,
                     layer, n_heads, qk_dim):
    b = pl.program_id(0)
    ch = pl.program_id(1)
    nb = pl.num_programs(0)
    nch = pl.num_programs(1)
    step = b * nch + ch
    slot = lax.rem(step, 2)
    n_rows = wukt_ref.shape[0]

    def copies(bb, cc, sl):
        out = []
        for i in range(PAGES_PER_STEP):
            pg = pt_ref[bb, cc * PAGES_PER_STEP + i]
            toks = pl.ds(i * PAGE, PAGE)
            out.append(pltpu.make_async_copy(ckv_hbm.at[layer, pg], cbuf.at[sl, toks],
                                             sem.at[0, sl]))
            out.append(pltpu.make_async_copy(krt_hbm.at[layer, pg], krbuf.at[sl, :, toks],
                                             sem.at[1, sl]))
        return out

    @pl.when(step == 0)
    def _():
        lhs_ref[0:n_rows] = wukt_ref[...]
        for cp in copies(0, 0, 0):
            cp.start()

    @pl.when(step + 1 < nb * nch)
    def _():
        wrap = ch + 1 == nch
        for cp in copies(jnp.where(wrap, b + 1, b), jnp.where(wrap, 0, ch + 1), 1 - slot):
            cp.start()

    @pl.when(ch == 0)
    def _():
        lhs_ref[n_rows:] = qlat_ref[0]
        m_ref[...] = snew_ref[0]
        l_ref[...] = jnp.ones_like(l_ref)
        acc_ref[...] = jnp.broadcast_to(cnew_ref[0], acc_ref.shape)

    for cp in copies(b, ch, slot):
        cp.wait()

    nope = n_rows // n_heads
    t = cbuf.shape[1]
    cb = cbuf[slot].astype(BF16)
    krt = krbuf[slot]
    kq = lax.dot_general(lhs_ref[...], cb, NT, preferred_element_type=F32)
    knt = kq[:n_rows]
    ss = jnp.sum((knt * knt).reshape(n_heads, nope, t), axis=1)
    ssr = jnp.sum(krt * krt, axis=0, keepdims=True)
    raw = kq[n_rows:] + jnp.dot(qrr_ref[0], krt.astype(BF16), preferred_element_type=F32)
    s = raw * lax.rsqrt((ss + ssr) / qk_dim + EPS)
    m_prev = m_ref[...]
    m_new = jnp.maximum(m_prev, jnp.max(s, axis=1, keepdims=True))
    corr = jnp.exp(m_prev - m_new)
    p = jnp.exp(s - m_new)
    l_ref[...] = l_ref[...] * corr + jnp.sum(p, axis=1, keepdims=True)
    acc_ref[...] = acc_ref[...] * corr + jnp.dot(p.astype(BF16), cb, preferred_element_type=F32)
    m_ref[...] = m_new

    @pl.when(ch == nch - 1)
    def _():
        o_ref[0] = acc_ref[...] / l_ref[...]


def _dec_attn(page_table, qlat, qrr, s_new, c_new, w_ukt, cache_ckv, cache_krt, *,
              layer, n_heads, qk_dim):
    b, n_pages = page_table.shape
    c = c_new.shape[-1]
    rope = cache_krt.shape[2]
    t = PAGES_PER_STEP * PAGE
    assert n_pages % PAGES_PER_STEP == 0
    assert cache_ckv.shape[2] == PAGE and cache_krt.shape[3] == PAGE
    per_b = lambda bi, ci, pt: (bi, 0, 0)
    grid_spec = pltpu.PrefetchScalarGridSpec(
        num_scalar_prefetch=1,
        grid=(b, n_pages // PAGES_PER_STEP),
        in_specs=[pl.BlockSpec((1, n_heads, c), per_b),
                  pl.BlockSpec((1, n_heads, rope), per_b),
                  pl.BlockSpec((1, n_heads, 1), per_b),
                  pl.BlockSpec((1, 1, c), per_b),
                  pl.BlockSpec(w_ukt.shape, lambda bi, ci, pt: (0, 0)),
                  pl.BlockSpec(memory_space=pl.ANY),
                  pl.BlockSpec(memory_space=pl.ANY)],
        out_specs=pl.BlockSpec((1, n_heads, c), per_b),
        scratch_shapes=[pltpu.VMEM((2, t, c), F32), pltpu.VMEM((2, rope, t), F32),
                        pltpu.VMEM((w_ukt.shape[0] + n_heads, c), BF16),
                        pltpu.SemaphoreType.DMA((2, 2)),
                        pltpu.VMEM((n_heads, 1), F32), pltpu.VMEM((n_heads, 1), F32),
                        pltpu.VMEM((n_heads, c), F32)])
    return pl.pallas_call(
        functools.partial(_dec_attn_kernel, layer=layer, n_heads=n_heads, qk_dim=qk_dim),
        grid_spec=grid_spec,
        out_shape=jax.ShapeDtypeStruct((b, n_heads, c), F32),
        compiler_params=_cp("arbitrary", "arbitrary"),
        name="dec_attn",
    )(page_table, qlat, qrr, s_new, c_new, w_ukt, cache_ckv, cache_krt)


def _dec_ov_kernel(o_ref, w_ref, out_ref):
    out_ref[...] = jnp.dot(o_ref[...].astype(BF16), w_ref[...],
                           preferred_element_type=F32).astype(out_ref.dtype)


def _dec_ov(o_lat2, w_uv, *, n_heads):
    b, hc = o_lat2.shape
    c = hc // n_heads
    vd = w_uv.shape[1] // n_heads
    return pl.pallas_call(
        _dec_ov_kernel,
        grid=(n_heads,),
        in_specs=[pl.BlockSpec((b, c), lambda j: (0, j)), pl.BlockSpec((c, vd), lambda j: (0, j))],
        out_specs=pl.BlockSpec((b, vd), lambda j: (0, j)),
        out_shape=jax.ShapeDtypeStruct((b, n_heads * vd), BF16),
        compiler_params=_cp("parallel"),
        name="dec_ov",
    )(o_lat2, w_uv)


def _lru_gates(xc, wa, ba, wx, bx, sp):
    xb = xc.astype(BF16)
    r = jax.nn.sigmoid(jnp.dot(xb, wa, preferred_element_type=F32) + ba)
    ig = jax.nn.sigmoid(jnp.dot(xb, wx, preferred_element_type=F32) + bx)
    log_a = (-LRU_C * r) * sp
    a = jnp.exp(log_a)
    bmul = jnp.sqrt(jnp.tanh(-log_a) * (a * a + 1.0))
    return a, bmul * (ig * xc)


def _softplus(x):
    return jnp.maximum(x, 0.0) + jnp.log1p(jnp.exp(-jnp.abs(x)))


def _lru_scan_kernel(u_ref, gate_ref, cw_ref, cb_ref, wa_ref, ba_ref, wx_ref, bx_ref, lam_ref,
                     y_ref, conv_ref, hlast_ref, halo_ref, hc_ref, *, bw):
    ti = pl.program_id(1)
    tb, ch = u_ref.shape

    @pl.when(ti == 0)
    def _():
        halo_ref[...] = jnp.zeros_like(halo_ref)
        hc_ref[...] = jnp.zeros_like(hc_ref)

    u = u_ref[...]
    ext = jnp.concatenate([halo_ref[...], u], axis=0)
    cw = cw_ref[...]
    xc = cb_ref[...] + (((cw[0:1] * ext[5:5 + tb] + cw[1:2] * ext[6:6 + tb])
                         + cw[2:3] * ext[7:7 + tb]) + cw[3:4] * u)
    halo_ref[...] = u[tb - 8:]
    sp = _softplus(-lam_ref[...])
    row = lax.broadcasted_iota(jnp.int32, (tb // 8, 8, bw), 1)
    for n in range(ch // bw):
        cs = slice(n * bw, (n + 1) * bw)
        a, bv = _lru_gates(xc[:, cs], wa_ref[n], ba_ref[:, cs], wx_ref[n], bx_ref[:, cs],
                           sp[:, cs])
        a = a.reshape(tb // 8, 8, bw)
        bv = bv.reshape(tb // 8, 8, bw)
        for sh in (1, 2, 4):
            keep = row >= sh
            bv = jnp.where(keep, a * pltpu.roll(bv, sh, 1) + bv, bv)
            a = jnp.where(keep, a * pltpu.roll(a, sh, 1), a)
        a = a.reshape(tb, bw)
        bv = bv.reshape(tb, bw)
        hprev = hc_ref[:, cs]
        hs = []
        for g in range(tb // 8):
            hg = a[g * 8:(g + 1) * 8] * hprev + bv[g * 8:(g + 1) * 8]
            hs.append(hg)
            hprev = hg[7:8]
        hc_ref[:, cs] = hprev
        y_ref[:, cs] = (jnp.concatenate(hs, axis=0) * gate_ref[:, cs]).astype(y_ref.dtype)

    @pl.when(ti == pl.num_programs(1) - 1)
    def _():
        conv_ref[0] = u[tb - 3:]
        hlast_ref[0] = hc_ref[...]


def _lru_scan(u, gate, conv_w, conv_b, w_a, b_a, w_x, b_x, lam, *, batch):
    m, ch = u.shape
    t = m // batch
    tb = _blk(t, 256)
    nt = t // tb
    nblk, bw, _ = w_a.shape
    kw = conv_w.shape[0]
    assert kw == 4 and tb % 8 == 0
    row = lambda bi, ti: (bi * nt + ti, 0)
    fixed2 = lambda bi, ti: (0, 0)
    fixed3 = lambda bi, ti: (0, 0, 0)
    per_b = lambda bi, ti: (bi, 0, 0)
    return pl.pallas_call(
        functools.partial(_lru_scan_kernel, bw=bw),
        grid=(batch, nt),
        in_specs=[pl.BlockSpec((tb, ch), row), pl.BlockSpec((tb, ch), row),
                  pl.BlockSpec((kw, ch), fixed2), pl.BlockSpec((1, ch), fixed2),
                  pl.BlockSpec((nblk, bw, bw), fixed3), pl.BlockSpec((1, ch), fixed2),
                  pl.BlockSpec((nblk, bw, bw), fixed3), pl.BlockSpec((1, ch), fixed2),
                  pl.BlockSpec((1, ch), fixed2)],
        out_specs=[pl.BlockSpec((tb, ch), row), pl.BlockSpec((1, kw - 1, ch), per_b),
                   pl.BlockSpec((1, 1, ch), per_b)],
        out_shape=[jax.ShapeDtypeStruct((m, ch), BF16),
                   jax.ShapeDtypeStruct((batch, kw - 1, ch), F32),
                   jax.ShapeDtypeStruct((batch, 1, ch), F32)],
        scratch_shapes=[pltpu.VMEM((8, ch), F32), pltpu.VMEM((1, ch), F32)],
        compiler_params=_cp("parallel", "arbitrary"),
        name="lru_scan",
    )(u, gate, conv_w, conv_b.reshape(1, ch), w_a, b_a.reshape(1, ch), w_x, b_x.reshape(1, ch),
      lam.reshape(1, ch))


def _lru_step_kernel(u_ref, gate_ref, s0_ref, s1_ref, s2_ref, h_ref, cw_ref, cb_ref, wa_ref,
                     ba_ref, wx_ref, bx_ref, lam_ref, y_ref, hn_ref):
    cw = cw_ref[...]
    u = u_ref[...]
    xc = cb_ref[...] + (((cw[0:1] * s0_ref[...] + cw[1:2] * s1_ref[...])
                         + cw[2:3] * s2_ref[...]) + cw[3:4] * u)
    a, bv = _lru_gates(xc, wa_ref[0], ba_ref[...], wx_ref[0], bx_ref[...],
                       _softplus(-lam_ref[...]))
    hn = a * h_ref[...] + bv
    hn_ref[...] = hn
    y_ref[...] = (hn * gate_ref[...]).astype(y_ref.dtype)


def _lru_step(u, gate, conv_state2, h_state, conv_w, conv_b, w_a, b_a, w_x, b_x, lam):
    b, ch = u.shape
    nblk, bw, _ = w_a.shape
    kw = conv_w.shape[0]
    assert kw == 4
    col = lambda j: (0, j)
    vec = pl.BlockSpec((1, bw), col)
    state = [pl.BlockSpec((b, bw), (lambda j, kk=kk: (0, kk * nblk + j))) for kk in range(kw - 1)]
    return pl.pallas_call(
        _lru_step_kernel,
        grid=(nblk,),
        in_specs=[pl.BlockSpec((b, bw), col), pl.BlockSpec((b, bw), col), *state,
                  pl.BlockSpec((b, bw), col), pl.BlockSpec((kw, bw), col), vec,
                  pl.BlockSpec((1, bw, bw), lambda j: (j, 0, 0)), vec,
                  pl.BlockSpec((1, bw, bw), lambda j: (j, 0, 0)), vec, vec],
        out_specs=[pl.BlockSpec((b, bw), col), pl.BlockSpec((b, bw), col)],
        out_shape=[jax.ShapeDtypeStruct((b, ch), BF16), jax.ShapeDtypeStruct((b, ch), F32)],
        compiler_params=_cp("parallel"),
        name="lru_step",
    )(u, gate, conv_state2, conv_state2, conv_state2, h_state, conv_w, conv_b.reshape(1, ch),
      w_a, b_a.reshape(1, ch), w_x, b_x.reshape(1, ch), lam.reshape(1, ch))


def _ffn(x, gain, w_gate, w_up, w_down):
    (h,) = _matmul(x, [w_gate, w_up], name="ffn_up", gain=gain, epilogue=_ep_swiglu,
                   out_dtypes=[BF16], bm=1024, bn=512)
    (y,) = _matmul(h, [w_down], name="ffn_down", res=x, epilogue=_ep_res, out_dtypes=[F32],
                   bm=1024, bn=512)
    return y


def kernel(x_prompt, x_sample, cache_ckv, cache_krope, page_table, state_conv, state_h, norm_mix, norm_ffn, mla_w_dq, mla_q_norm, mla_w_uq, mla_w_dkv, mla_kv_norm, mla_w_uk, mla_w_uv, mla_q_gain, mla_k_gain, mla_w_o, lru_w_gate, lru_w_in, lru_conv_w, lru_conv_b, lru_w_a, lru_b_a, lru_w_x, lru_b_x, lru_lambda, lru_w_out, ffn_w_gate, ffn_w_up, ffn_w_down):
    bp, sp, d = x_prompt.shape
    bs, ss, _ = x_sample.shape
    assert ss == 1, "sample path handles one new token per sequence"
    depth = norm_mix.shape[0]
    n_heads, nope = mla_w_uk.shape[2], mla_w_uk.shape[3]
    q_lora, kv_lora = mla_w_dq.shape[2], mla_w_uk.shape[1]
    qk = mla_w_uq.shape[3]
    rope = qk - nope
    vd = mla_w_uv.shape[3]
    past = page_table.shape[1] * cache_ckv.shape[2]
    scale = float(qk) ** -0.5

    xp = x_prompt.reshape(bp * sp, d)
    xs = x_sample.reshape(bs, d)
    cos_p, sin_p = _rope_tables(jnp.arange(sp), rope)
    cos_s, sin_s = _rope_tables(jnp.full((bs,), past), rope)
    cache_krt = jnp.swapaxes(cache_krope, 2, 3)

    ckv_p, kr_p, ckv_s, kr_s = [], [], [], []
    conv_p, h_p, conv_s, h_s = [], [], [], []
    for i in range(depth):
        j = i // 2
        if i % 2 == 0:
            w_down = jnp.concatenate([mla_w_dq[j], mla_w_dkv[j]], axis=1).astype(BF16)
            w_uq = mla_w_uq[j]
            w_uq_h = jnp.transpose(w_uq, (1, 0, 2)).astype(BF16)
            w_uq_split = jnp.concatenate(
                [w_uq[:, :, :nope].reshape(q_lora, n_heads * nope),
                 w_uq[:, :, nope:].reshape(q_lora, n_heads * rope)], axis=1).astype(BF16)
            w_uk = mla_w_uk[j].reshape(kv_lora, n_heads * nope).astype(BF16)
            w_uv = mla_w_uv[j].reshape(kv_lora, n_heads * vd).astype(BF16)
            w_o = mla_w_o[j].reshape(n_heads * vd, d).astype(BF16)
            down = functools.partial(_qkv_down, gain=norm_mix[i], w_cat=w_down,
                                     q_norm=mla_q_norm[j], kv_norm=mla_kv_norm[j],
                                     q_lora=q_lora, kv_lora=kv_lora)
            qn, c1, cb1, r1 = down(xp, cos2=cos_p, sin2=sin_p)
            q = _q_up(qn, w_uq_split, jnp.tile(cos_p, (1, n_heads)), jnp.tile(sin_p, (1, n_heads)),
                      mla_q_gain[j], batch=bp, n_heads=n_heads, nope=nope, scale=scale * LOG2E)
            k, vt = _kv_up(cb1, r1, w_uk, w_uv.T, mla_k_gain[j], batch=bp, n_heads=n_heads,
                           nope=nope)
            o = _flash(q, k, vt).reshape(bp * sp, n_heads * vd)
            (xp,) = _matmul(o, [w_o], name="mla_out", res=xp, epilogue=_ep_res, out_dtypes=[F32])
            qn, c2, _, r2 = down(xs, cos2=cos_s, sin2=sin_s)
            qlat, qrr, s_new = _dec_q(qn, w_uq_h, cos_s, sin_s, mla_q_gain[j], mla_k_gain[j],
                                      w_uk, c2, r2, nope=nope, scale=scale)
            o_lat = _dec_attn(page_table, qlat.reshape(bs, n_heads, kv_lora),
                              jnp.transpose(qrr, (1, 0, 2)),
                              s_new[:, :n_heads].reshape(bs, n_heads, 1),
                              c2.reshape(bs, 1, kv_lora), w_uk.T, cache_ckv, cache_krt,
                              layer=j, n_heads=n_heads, qk_dim=qk)
            o = _dec_ov(o_lat.reshape(bs, n_heads * kv_lora), w_uv, n_heads=n_heads)
            (xs,) = _matmul(o, [w_o], name="mla_out", res=xs, epilogue=_ep_res, out_dtypes=[F32])
            ckv_p.append(c1.reshape(bp, sp, kv_lora)); kr_p.append(r1.reshape(bp, sp, rope))
            ckv_s.append(c2.reshape(bs, 1, kv_lora)); kr_s.append(r2.reshape(bs, 1, rope))
        else:
            w_gate = lru_w_gate[j].astype(BF16)
            w_in = lru_w_in[j].astype(BF16)
            w_out = lru_w_out[j].astype(BF16)
            w_a = lru_w_a[j].astype(BF16)
            w_x = lru_w_x[j].astype(BF16)
            rest = (lru_conv_w[j], lru_conv_b[j], w_a, lru_b_a[j].reshape(-1), w_x,
                    lru_b_x[j].reshape(-1), lru_lambda[j])
            gate, u = _matmul(xp, [w_gate, w_in], name="lru_in", gain=norm_mix[i],
                              epilogue=_ep_gelu_id, out_dtypes=[F32, F32], bm=1024, bn=512)
            y, cv1, hh1 = _lru_scan(u, gate, *rest, batch=bp)
            (xp,) = _matmul(y, [w_out], name="lru_out", res=xp, epilogue=_ep_res, out_dtypes=[F32])
            gate, u = _matmul(xs, [w_gate, w_in], name="lru_in", gain=norm_mix[i],
                              epilogue=_ep_gelu_id, out_dtypes=[F32, F32], bm=1024, bn=512)
            y, hh2 = _lru_step(u, gate, state_conv[j].reshape(bs, -1), state_h[j], *rest)
            cv2 = jnp.concatenate([state_conv[j][:, 1:], u[:, None, :]], axis=1)
            (xs,) = _matmul(y, [w_out], name="lru_out", res=xs, epilogue=_ep_res, out_dtypes=[F32])
            conv_p.append(cv1); h_p.append(hh1.reshape(bp, -1)); conv_s.append(cv2); h_s.append(hh2)
        wg, wu, wd = (ffn_w_gate[i].astype(BF16), ffn_w_up[i].astype(BF16),
                      ffn_w_down[i].astype(BF16))
        xp = _ffn(xp, norm_ffn[i], wg, wu, wd)
        xs = _ffn(xs, norm_ffn[i], wg, wu, wd)
    return (xp.reshape(bp, sp, d), xs.reshape(bs, ss, d),
            jnp.stack(ckv_p), jnp.stack(kr_p), jnp.stack(ckv_s), jnp.stack(kr_s),
            jnp.stack(conv_p), jnp.stack(h_p), jnp.stack(conv_s), jnp.stack(h_s))
```

```python
import functools

import jax
import jax.numpy as jnp
from jax import lax
from jax.experimental import pallas as pl
from jax.experimental.pallas import tpu as pltpu

F32 = jnp.float32
BF16 = jnp.bfloat16

EPS = 1e-6
NEG = -1e30
ROPE_BASE = 10000.0
LRU_C = 8.0
PAGE = 128
PAGES_PER_STEP = 64
FLASH_BQ = 1024
LOG2E = 1.4426950408889634
VMEM_LIMIT = 56 * 1024 * 1024
NT = (((1,), (1,)), ((), ()))


def _cp(*sem):
    return pltpu.CompilerParams(dimension_semantics=sem, vmem_limit_bytes=VMEM_LIMIT)


def _blk(dim, pref):
    b = min(dim, pref)
    while dim % b:
        b //= 2
    return b


def _rms(xf, gain):
    ms = jnp.mean(xf * xf, axis=-1, keepdims=True)
    return (xf * lax.rsqrt(ms + EPS)) * gain


def _mm_kernel(*refs, n_w, has_norm, has_res, n_out, epilogue):
    refs = list(refs)
    x_ref = refs.pop(0)
    g_ref = refs.pop(0) if has_norm else None
    w_refs = [refs.pop(0) for _ in range(n_w)]
    res_ref = refs.pop(0) if has_res else None
    o_refs = [refs.pop(0) for _ in range(n_out)]
    if has_norm:
        xn_ref = refs.pop(0)

        @pl.when(pl.program_id(1) == 0)
        def _():
            xn_ref[...] = _rms(x_ref[...], g_ref[...]).astype(BF16)

        xb = xn_ref[...]
    else:
        xb = x_ref[...]
    accs = [jnp.dot(xb, w[...], preferred_element_type=F32) for w in w_refs]
    outs = epilogue(accs, res_ref[...] if has_res else None)
    for o_ref, o in zip(o_refs, outs):
        o_ref[...] = o.astype(o_ref.dtype)


def _matmul(x, ws, *, name, epilogue, out_dtypes, gain=None, res=None, bm=1024, bn=1024):
    m, k = x.shape
    n = ws[0].shape[1]
    bm, bn = _blk(m, bm), _blk(n, bn)
    in_specs = [pl.BlockSpec((bm, k), lambda i, j: (i, 0))]
    args = [x]
    if gain is not None:
        in_specs.append(pl.BlockSpec((1, k), lambda i, j: (0, 0)))
        args.append(gain.reshape(1, k))
    for w in ws:
        in_specs.append(pl.BlockSpec((k, bn), lambda i, j: (0, j)))
        args.append(w)
    if res is not None:
        in_specs.append(pl.BlockSpec((bm, bn), lambda i, j: (i, j)))
        args.append(res)
    kern = functools.partial(_mm_kernel, n_w=len(ws), has_norm=gain is not None,
                             has_res=res is not None, n_out=len(out_dtypes),
                             epilogue=epilogue)
    outs = pl.pallas_call(
        kern,
        grid=(m // bm, n // bn),
        in_specs=in_specs,
        out_specs=[pl.BlockSpec((bm, bn), lambda i, j: (i, j)) for _ in out_dtypes],
        out_shape=[jax.ShapeDtypeStruct((m, n), dt) for dt in out_dtypes],
        scratch_shapes=[pltpu.VMEM((bm, k), BF16)] if gain is not None else [],
        compiler_params=_cp("parallel", "arbitrary"),
        name=name,
    )(*args)
    return outs


def _ep_res(accs, res):
    return [res + accs[0]]


def _ep_swiglu(accs, res):
    g, u = accs
    return [(g * jax.nn.sigmoid(g)) * u]


def _ep_gelu_id(accs, res):
    return [jax.nn.gelu(accs[0]), accs[1]]


def _rope_tables(pos, rope_dim):
    half = rope_dim // 2
    inv = ROPE_BASE ** (-jnp.arange(half, dtype=F32) / half)
    ang = pos.astype(F32)[:, None] * inv[None, :]
    cos, sin = jnp.cos(ang), jnp.sin(ang)
    return jnp.concatenate([cos, cos], -1), jnp.concatenate([-sin, sin], -1)


def _rope(x, cos2, sin2):
    half = x.shape[-1] // 2
    swapped = jnp.concatenate([x[:, half:], x[:, :half]], axis=-1)
    return x * cos2 + swapped * sin2


def _swap_halves(x, half):
    lane = lax.broadcasted_iota(jnp.int32, x.shape, 1)
    n = x.shape[1]
    return jnp.where(lane % (2 * half) < half, pltpu.roll(x, n - half, 1), pltpu.roll(x, half, 1))


def _qkv_down_kernel(x_ref, g_ref, w_ref, qn_g_ref, kvn_g_ref, cos_ref, sin_ref,
                     qn_ref, c_ref, cb_ref, kr_ref, *, q_lora, kv_lora):
    hb = _rms(x_ref[...], g_ref[...]).astype(BF16)
    a = jnp.dot(hb, w_ref[...], preferred_element_type=F32)
    qn_ref[...] = _rms(a[:, :q_lora], qn_g_ref[...]).astype(BF16)
    c = _rms(a[:, q_lora:q_lora + kv_lora], kvn_g_ref[...])
    c_ref[...] = c
    cb_ref[...] = c.astype(BF16)
    kr_ref[...] = _rope(a[:, q_lora + kv_lora:], cos_ref[...], sin_ref[...])


def _qkv_down(x, gain, w_cat, q_norm, kv_norm, cos2, sin2, *, q_lora, kv_lora):
    m, d = x.shape
    n = w_cat.shape[1]
    rope = n - q_lora - kv_lora
    bm = _blk(m, 512)
    nt = cos2.shape[0] // bm
    row = lambda i: (i, 0)
    fixed = lambda i: (0, 0)
    tab = lambda i: (i % nt, 0)
    return pl.pallas_call(
        functools.partial(_qkv_down_kernel, q_lora=q_lora, kv_lora=kv_lora),
        grid=(m // bm,),
        in_specs=[pl.BlockSpec((bm, d), row), pl.BlockSpec((1, d), fixed),
                  pl.BlockSpec((d, n), fixed), pl.BlockSpec((1, q_lora), fixed),
                  pl.BlockSpec((1, kv_lora), fixed),
                  pl.BlockSpec((bm, rope), tab), pl.BlockSpec((bm, rope), tab)],
        out_specs=[pl.BlockSpec((bm, q_lora), row), pl.BlockSpec((bm, kv_lora), row),
                   pl.BlockSpec((bm, kv_lora), row), pl.BlockSpec((bm, rope), row)],
        out_shape=[jax.ShapeDtypeStruct((m, q_lora), BF16),
                   jax.ShapeDtypeStruct((m, kv_lora), F32),
                   jax.ShapeDtypeStruct((m, kv_lora), BF16),
                   jax.ShapeDtypeStruct((m, rope), F32)],
        compiler_params=_cp("parallel"),
        name="qkv_down",
    )(x, gain.reshape(1, d), w_cat, q_norm.reshape(1, -1), kv_norm.reshape(1, -1), cos2, sin2)


def _q_up_kernel(qn_ref, w_ref, cos_ref, sin_ref, g_ref, e_ref, et_ref, q_ref, *, heads, nope):
    a = jnp.dot(qn_ref[...], w_ref[...], preferred_element_type=F32)
    hn = heads * nope
    rope = (a.shape[1] - hn) // heads
    qk = nope + rope
    r = a[:, hn:]
    r = r * cos_ref[...] + _swap_halves(r, rope // 2) * sin_ref[...]
    x = jnp.concatenate([a[:, :hn], r], axis=1)
    ss = jnp.dot((x * x).astype(BF16), e_ref[...], preferred_element_type=F32)
    inv = lax.rsqrt(ss / qk + EPS)
    hi = inv.astype(BF16)
    lo = (inv - hi.astype(F32)).astype(BF16)
    invb = jnp.dot(jnp.concatenate([hi, lo], axis=1), et_ref[...], preferred_element_type=F32)
    y = ((x * invb) * g_ref[...]).astype(BF16)
    for i in range(heads):
        q_ref[0, i, :, :nope] = y[:, i * nope:(i + 1) * nope]
        q_ref[0, i, :, nope:] = y[:, hn + i * rope:hn + (i + 1) * rope]


def _q_up(qn, w_uq_split, cos_t, sin_t, q_gain, *, batch, n_heads, nope, scale):
    m, r = qn.shape
    n = w_uq_split.shape[1]
    qk = n // n_heads
    rope = qk - nope
    s = m // batch
    bm = _blk(s, 512)
    ns = s // bm
    lanes = 128
    assert n_heads <= lanes
    head_of_col = jnp.concatenate([jnp.arange(n_heads * nope) // nope,
                                   jnp.arange(n_heads * rope) // rope])
    e = (head_of_col[:, None] == jnp.arange(lanes)[None, :]).astype(BF16)
    et2 = jnp.concatenate([e.T, e.T], axis=0)
    g_cols = jnp.concatenate([jnp.tile(q_gain[:nope], n_heads), jnp.tile(q_gain[nope:], n_heads)])
    fixed = lambda i: (0, 0)
    tab = lambda i: (i % ns, 0)
    return pl.pallas_call(
        functools.partial(_q_up_kernel, heads=n_heads, nope=nope),
        grid=(m // bm,),
        in_specs=[pl.BlockSpec((bm, r), lambda i: (i, 0)), pl.BlockSpec((r, n), fixed),
                  pl.BlockSpec((bm, n_heads * rope), tab), pl.BlockSpec((bm, n_heads * rope), tab),
                  pl.BlockSpec((1, n), fixed), pl.BlockSpec((n, lanes), fixed),
                  pl.BlockSpec((2 * lanes, n), fixed)],
        out_specs=pl.BlockSpec((1, n_heads, bm, qk), lambda i: (i // ns, 0, i % ns, 0)),
        out_shape=jax.ShapeDtypeStruct((batch, n_heads, s, qk), BF16),
        compiler_params=_cp("parallel"),
        name="q_up",
    )(qn, w_uq_split, cos_t, sin_t, (g_cols * scale).reshape(1, n), e, et2)


def _kv_up_kernel(cb_ref, kr_ref, wuk_ref, wuvt_ref, g_ref, k_ref, vt_ref, *, heads, nope):
    cb = cb_ref[...]
    kr = kr_ref[...]
    qk = nope + kr.shape[-1]
    kn = jnp.dot(cb, wuk_ref[...], preferred_element_type=F32)
    vt = lax.dot_general(wuvt_ref[...], cb, NT, preferred_element_type=F32)
    ssr = jnp.sum(kr * kr, axis=-1, keepdims=True)
    g = g_ref[...]
    vd = vt.shape[0] // heads
    for i in range(heads):
        kh = kn[:, i * nope:(i + 1) * nope]
        inv = lax.rsqrt((jnp.sum(kh * kh, axis=-1, keepdims=True) + ssr) / qk + EPS)
        k_ref[0, i, :, :nope] = ((kh * inv) * g[:, :nope]).astype(BF16)
        k_ref[0, i, :, nope:] = ((kr * inv) * g[:, nope:]).astype(BF16)
        vt_ref[0, i] = vt[i * vd:(i + 1) * vd].astype(BF16)


def _kv_up(cb, kr, w_uk, w_uvt, k_gain, *, batch, n_heads, nope):
    m, c = cb.shape
    rope = kr.shape[1]
    qk = nope + rope
    vd = w_uvt.shape[0] // n_heads
    s = m // batch
    bm = _blk(s, 512)
    ns = s // bm
    return pl.pallas_call(
        functools.partial(_kv_up_kernel, heads=n_heads, nope=nope),
        grid=(m // bm,),
        in_specs=[pl.BlockSpec((bm, c), lambda i: (i, 0)),
                  pl.BlockSpec((bm, rope), lambda i: (i, 0)),
                  pl.BlockSpec(w_uk.shape, lambda i: (0, 0)),
                  pl.BlockSpec(w_uvt.shape, lambda i: (0, 0)),
                  pl.BlockSpec((1, qk), lambda i: (0, 0))],
        out_specs=[pl.BlockSpec((1, n_heads, bm, qk), lambda i: (i // ns, 0, i % ns, 0)),
                   pl.BlockSpec((1, n_heads, vd, bm), lambda i: (i // ns, 0, 0, i % ns))],
        out_shape=[jax.ShapeDtypeStruct((batch, n_heads, s, qk), BF16),
                   jax.ShapeDtypeStruct((batch, n_heads, vd, s), BF16)],
        compiler_params=_cp("parallel"),
        name="kv_up",
    )(cb, kr, w_uk, w_uvt, k_gain.reshape(1, qk))


def _flash_kernel(q_ref, k_ref, vt_ref, o_ref, *, bq):
    s_len = q_ref.shape[2]
    kpos = lax.broadcasted_iota(jnp.int32, (bq, bq), 0)
    qpos = lax.broadcasted_iota(jnp.int32, (bq, bq), 1)
    for qi in range(s_len // bq):
        nk = (qi + 1) * bq
        q = q_ref[0, 0, qi * bq:nk, :]
        st = lax.dot_general(k_ref[0, 0, 0:nk, :], q, NT, preferred_element_type=F32)
        diag = jnp.where(kpos <= qpos, st[nk - bq:], NEG)
        st = diag if qi == 0 else jnp.concatenate([st[:nk - bq], diag], axis=0)
        m = jnp.max(st, axis=0, keepdims=True)
        p = jnp.exp2(st - m)
        l = jnp.sum(p, axis=0, keepdims=True)
        acct = jnp.dot(vt_ref[0, 0, :, 0:nk], p.astype(BF16), preferred_element_type=F32)
        o_ref[0, qi * bq:nk, :] = jnp.transpose(acct / l).astype(o_ref.dtype)


def _flash(q, k, vt):
    b, h, s, qk = q.shape
    vd = vt.shape[2]
    bq = _blk(s, FLASH_BQ)
    head = lambda bi, hi: (bi, hi, 0, 0)
    return pl.pallas_call(
        functools.partial(_flash_kernel, bq=bq),
        grid=(b, h),
        in_specs=[pl.BlockSpec((1, 1, s, qk), head), pl.BlockSpec((1, 1, s, qk), head),
                  pl.BlockSpec((1, 1, vd, s), head)],
        out_specs=pl.BlockSpec((1, s, vd), lambda bi, hi: (bi, 0, hi)),
        out_shape=jax.ShapeDtypeStruct((b, s, h * vd), BF16),
        compiler_params=_cp("parallel", "parallel"),
        name="flash",
    )(q, k, vt)


def _dec_q_kernel(qn_ref, wq_ref, cos_ref, sin_ref, qg_ref, kg_ref, wuk_ref, c_ref, kr_ref,
                  qlat_ref, qrr_ref, s_ref, *, nope, scale):
    hd = pl.program_id(0)
    q = jnp.dot(qn_ref[...], wq_ref[0], preferred_element_type=F32)
    q = jnp.concatenate([q[:, :nope], _rope(q[:, nope:], cos_ref[...], sin_ref[...])], axis=-1)
    qg = (_rms(q, qg_ref[...]) * kg_ref[...]) * scale
    wuk = wuk_ref[...]
    qlat = lax.dot_general(qg[:, :nope].astype(BF16), wuk, NT, preferred_element_type=F32)
    qrr = qg[:, nope:]
    qlat_ref[...] = qlat.astype(BF16)
    qrr_ref[0] = qrr.astype(BF16)
    c = c_ref[...]
    kr = kr_ref[...]
    kn = jnp.dot(c.astype(BF16), wuk, preferred_element_type=F32)
    ss = jnp.sum(kn * kn, axis=-1, keepdims=True) + jnp.sum(kr * kr, axis=-1, keepdims=True)
    inv = lax.rsqrt(ss / (nope + kr.shape[-1]) + EPS)
    raw = (jnp.sum(qlat * c, axis=-1, keepdims=True)
           + jnp.sum(qrr * kr, axis=-1, keepdims=True))
    lane = lax.broadcasted_iota(jnp.int32, s_ref.shape, 1)

    @pl.when(hd == 0)
    def _():
        s_ref[...] = jnp.zeros_like(s_ref)

    s_ref[...] = jnp.where(lane == hd, raw * inv, s_ref[...])


def _dec_q(qn, w_uq_h, cos2, sin2, q_gain, k_gain, w_uk, c_new, kr_new, *, nope, scale):
    b, r = qn.shape
    h, _, qk = w_uq_h.shape
    c = c_new.shape[1]
    rope = qk - nope
    fixed = lambda j: (0, 0)
    return pl.pallas_call(
        functools.partial(_dec_q_kernel, nope=nope, scale=scale),
        grid=(h,),
        in_specs=[pl.BlockSpec((b, r), fixed), pl.BlockSpec((1, r, qk), lambda j: (j, 0, 0)),
                  pl.BlockSpec((b, rope), fixed), pl.BlockSpec((b, rope), fixed),
                  pl.BlockSpec((1, qk), fixed), pl.BlockSpec((1, qk), fixed),
                  pl.BlockSpec((c, nope), lambda j: (0, j)),
                  pl.BlockSpec((b, c), fixed), pl.BlockSpec((b, rope), fixed)],
        out_specs=[pl.BlockSpec((b, c), lambda j: (0, j)),
                   pl.BlockSpec((1, b, rope), lambda j: (j, 0, 0)),
                   pl.BlockSpec((b, 128), fixed)],
        out_shape=[jax.ShapeDtypeStruct((b, h * c), BF16),
                   jax.ShapeDtypeStruct((h, b, rope), BF16),
                   jax.ShapeDtypeStruct((b, 128), F32)],
        compiler_params=_cp("arbitrary"),
        name="dec_q",
    )(qn, w_uq_h, cos2, sin2, q_gain.reshape(1, qk), k_gain.reshape(1, qk), w_uk, c_new, kr_new)


def _dec_attn_kernel(pt_ref, qlat_ref, qrr_ref, snew_ref, cnew_ref, wukt_ref, ckv_hbm, krt_hbm,
                     o_ref, cbuf, krbuf, lhs_ref, sem, m_ref, l_ref, acc_ref, *,
                     layer, n_heads, qk_dim):
    b = pl.program_id(0)
    ch = pl.program_id(1)
    nb = pl.num_programs(0)
    nch = pl.num_programs(1)
    step = b * nch + ch
    slot = lax.rem(step, 2)
    n_rows = wukt_ref.shape[0]

    def copies(bb, cc, sl):
        out = []
        for i in range(PAGES_PER_STEP):
            pg = pt_ref[bb, cc * PAGES_PER_STEP + i]
            toks = pl.ds(i * PAGE, PAGE)
            out.append(pltpu.make_async_copy(ckv_hbm.at[layer, pg], cbuf.at[sl, toks],
                                             sem.at[0, sl]))
            out.append(pltpu.make_async_copy(krt_hbm.at[layer, pg], krbuf.at[sl, :, toks],
                                             sem.at[1, sl]))
        return out

    @pl.when(step == 0)
    def _():
        lhs_ref[0:n_rows] = wukt_ref[...]
        for cp in copies(0, 0, 0):
            cp.start()

    @pl.when(step + 1 < nb * nch)
    def _():
        wrap = ch + 1 == nch
        for cp in copies(jnp.where(wrap, b + 1, b), jnp.where(wrap, 0, ch + 1), 1 - slot):
            cp.start()

    @pl.when(ch == 0)
    def _():
        lhs_ref[n_rows:] = qlat_ref[0]
        m_ref[...] = snew_ref[0]
        l_ref[...] = jnp.ones_like(l_ref)
        acc_ref[...] = jnp.broadcast_to(cnew_ref[0], acc_ref.shape)

    pltpu.make_async_copy(cbuf.at[slot], cbuf.at[slot], sem.at[0, slot]).wait()
    pltpu.make_async_copy(krbuf.at[slot], krbuf.at[slot], sem.at[1, slot]).wait()

    nope = n_rows // n_heads
    t = cbuf.shape[1]
    cb = cbuf[slot].astype(BF16)
    krt = krbuf[slot]
    kq = lax.dot_general(lhs_ref[...], cb, NT, preferred_element_type=F32)
    knt = kq[:n_rows]
    ss = jnp.sum((knt * knt).reshape(n_heads, nope, t), axis=1)
    ssr = jnp.sum(krt * krt, axis=0, keepdims=True)
    raw = kq[n_rows:] + jnp.dot(qrr_ref[0], krt.astype(BF16), preferred_element_type=F32)
    s = raw * lax.rsqrt((ss + ssr) / qk_dim + EPS)
    m_prev = m_ref[...]
    m_new = jnp.maximum(m_prev, jnp.max(s, axis=1, keepdims=True))
    corr = jnp.exp(m_prev - m_new)
    p = jnp.exp(s - m_new)
    l_ref[...] = l_ref[...] * corr + jnp.sum(p, axis=1, keepdims=True)
    acc_ref[...] = acc_ref[...] * corr + jnp.dot(p.astype(BF16), cb, preferred_element_type=F32)
    m_ref[...] = m_new

    @pl.when(ch == nch - 1)
    def _():
        o_ref[0] = acc_ref[...] / l_ref[...]


def _dec_attn(page_table, qlat, qrr, s_new, c_new, w_ukt, cache_ckv, cache_krt, *,
              layer, n_heads, qk_dim):
    b, n_pages = page_table.shape
    c = c_new.shape[-1]
    rope = cache_krt.shape[2]
    t = PAGES_PER_STEP * PAGE
    assert n_pages % PAGES_PER_STEP == 0
    assert cache_ckv.shape[2] == PAGE and cache_krt.shape[3] == PAGE
    per_b = lambda bi, ci, pt: (bi, 0, 0)
    grid_spec = pltpu.PrefetchScalarGridSpec(
        num_scalar_prefetch=1,
        grid=(b, n_pages // PAGES_PER_STEP),
        in_specs=[pl.BlockSpec((1, n_heads, c), per_b),
                  pl.BlockSpec((1, n_heads, rope), per_b),
                  pl.BlockSpec((1, n_heads, 1), per_b),
                  pl.BlockSpec((1, 1, c), per_b),
                  pl.BlockSpec(w_ukt.shape, lambda bi, ci, pt: (0, 0)),
                  pl.BlockSpec(memory_space=pl.ANY),
                  pl.BlockSpec(memory_space=pl.ANY)],
        out_specs=pl.BlockSpec((1, n_heads, c), per_b),
        scratch_shapes=[pltpu.VMEM((2, t, c), F32), pltpu.VMEM((2, rope, t), F32),
                        pltpu.VMEM((w_ukt.shape[0] + n_heads, c), BF16),
                        pltpu.SemaphoreType.DMA((2, 2)),
                        pltpu.VMEM((n_heads, 1), F32), pltpu.VMEM((n_heads, 1), F32),
                        pltpu.VMEM((n_heads, c), F32)])
    return pl.pallas_call(
        functools.partial(_dec_attn_kernel, layer=layer, n_heads=n_heads, qk_dim=qk_dim),
        grid_spec=grid_spec,
        out_shape=jax.ShapeDtypeStruct((b, n_heads, c), F32),
        compiler_params=_cp("arbitrary", "arbitrary"),
        name="dec_attn",
    )(page_table, qlat, qrr, s_new, c_new, w_ukt, cache_ckv, cache_krt)


def _dec_ov_kernel(o_ref, w_ref, out_ref):
    out_ref[...] = jnp.dot(o_ref[...].astype(BF16), w_ref[...],
                           preferred_element_type=F32).astype(out_ref.dtype)


def _dec_ov(o_lat2, w_uv, *, n_heads):
    b, hc = o_lat2.shape
    c = hc // n_heads
    vd = w_uv.shape[1] // n_heads
    return pl.pallas_call(
        _dec_ov_kernel,
        grid=(n_heads,),
        in_specs=[pl.BlockSpec((b, c), lambda j: (0, j)), pl.BlockSpec((c, vd), lambda j: (0, j))],
        out_specs=pl.BlockSpec((b, vd), lambda j: (0, j)),
        out_shape=jax.ShapeDtypeStruct((b, n_heads * vd), BF16),
        compiler_params=_cp("parallel"),
        name="dec_ov",
    )(o_lat2, w_uv)


def _lru_gates(xc, wa, ba, wx, bx, sp):
    xb = xc.astype(BF16)
    r = jax.nn.sigmoid(jnp.dot(xb, wa, preferred_element_type=F32) + ba)
    ig = jax.nn.sigmoid(jnp.dot(xb, wx, preferred_element_type=F32) + bx)
    log_a = (-LRU_C * r) * sp
    a = jnp.exp(log_a)
    bmul = jnp.sqrt(jnp.tanh(-log_a) * (a * a + 1.0))
    return a, bmul * (ig * xc)


def _softplus(x):
    return jnp.maximum(x, 0.0) + jnp.log1p(jnp.exp(-jnp.abs(x)))


def _lru_scan_kernel(u_ref, gate_ref, cw_ref, cb_ref, wa_ref, ba_ref, wx_ref, bx_ref, lam_ref,
                     y_ref, conv_ref, hlast_ref, halo_ref, hc_ref, *, bw):
    ti = pl.program_id(1)
    tb, ch = u_ref.shape

    @pl.when(ti == 0)
    def _():
        halo_ref[...] = jnp.zeros_like(halo_ref)
        hc_ref[...] = jnp.zeros_like(hc_ref)

    u = u_ref[...]
    ext = jnp.concatenate([halo_ref[...], u], axis=0)
    cw = cw_ref[...]
    xc = cb_ref[...] + (((cw[0:1] * ext[5:5 + tb] + cw[1:2] * ext[6:6 + tb])
                         + cw[2:3] * ext[7:7 + tb]) + cw[3:4] * u)
    halo_ref[...] = u[tb - 8:]
    sp = _softplus(-lam_ref[...])
    row = lax.broadcasted_iota(jnp.int32, (tb // 8, 8, bw), 1)
    for n in range(ch // bw):
        cs = slice(n * bw, (n + 1) * bw)
        a, bv = _lru_gates(xc[:, cs], wa_ref[n], ba_ref[:, cs], wx_ref[n], bx_ref[:, cs],
                           sp[:, cs])
        a = a.reshape(tb // 8, 8, bw)
        bv = bv.reshape(tb // 8, 8, bw)
        for sh in (1, 2, 4):
            keep = row >= sh
            bv = jnp.where(keep, a * pltpu.roll(bv, sh, 1) + bv, bv)
            a = jnp.where(keep, a * pltpu.roll(a, sh, 1), a)
        a = a.reshape(tb, bw)
        bv = bv.reshape(tb, bw)
        hprev = hc_ref[:, cs]
        hs = []
        for g in range(tb // 8):
            hg = a[g * 8:(g + 1) * 8] * hprev + bv[g * 8:(g + 1) * 8]
            hs.append(hg)
            hprev = hg[7:8]
        hc_ref[:, cs] = hprev
        y_ref[:, cs] = (jnp.concatenate(hs, axis=0) * gate_ref[:, cs]).astype(y_ref.dtype)

    @pl.when(ti == pl.num_programs(1) - 1)
    def _():
        conv_ref[0] = u[tb - 3:]
        hlast_ref[0] = hc_ref[...]


def _lru_scan(u, gate, conv_w, conv_b, w_a, b_a, w_x, b_x, lam, *, batch):
    m, ch = u.shape
    t = m // batch
    tb = _blk(t, 256)
    nt = t // tb
    nblk, bw, _ = w_a.shape
    kw = conv_w.shape[0]
    assert kw == 4 and tb % 8 == 0
    row = lambda bi, ti: (bi * nt + ti, 0)
    fixed2 = lambda bi, ti: (0, 0)
    fixed3 = lambda bi, ti: (0, 0, 0)
    per_b = lambda bi, ti: (bi, 0, 0)
    return pl.pallas_call(
        functools.partial(_lru_scan_kernel, bw=bw),
        grid=(batch, nt),
        in_specs=[pl.BlockSpec((tb, ch), row), pl.BlockSpec((tb, ch), row),
                  pl.BlockSpec((kw, ch), fixed2), pl.BlockSpec((1, ch), fixed2),
                  pl.BlockSpec((nblk, bw, bw), fixed3), pl.BlockSpec((1, ch), fixed2),
                  pl.BlockSpec((nblk, bw, bw), fixed3), pl.BlockSpec((1, ch), fixed2),
                  pl.BlockSpec((1, ch), fixed2)],
        out_specs=[pl.BlockSpec((tb, ch), row), pl.BlockSpec((1, kw - 1, ch), per_b),
                   pl.BlockSpec((1, 1, ch), per_b)],
        out_shape=[jax.ShapeDtypeStruct((m, ch), BF16),
                   jax.ShapeDtypeStruct((batch, kw - 1, ch), F32),
                   jax.ShapeDtypeStruct((batch, 1, ch), F32)],
        scratch_shapes=[pltpu.VMEM((8, ch), F32), pltpu.VMEM((1, ch), F32)],
        compiler_params=_cp("parallel", "arbitrary"),
        name="lru_scan",
    )(u, gate, conv_w, conv_b.reshape(1, ch), w_a, b_a.reshape(1, ch), w_x, b_x.reshape(1, ch),
      lam.reshape(1, ch))


def _lru_step_kernel(u_ref, gate_ref, s0_ref, s1_ref, s2_ref, h_ref, cw_ref, cb_ref, wa_ref,
                     ba_ref, wx_ref, bx_ref, lam_ref, y_ref, hn_ref):
    cw = cw_ref[...]
    u = u_ref[...]
    xc = cb_ref[...] + (((cw[0:1] * s0_ref[...] + cw[1:2] * s1_ref[...])
                         + cw[2:3] * s2_ref[...]) + cw[3:4] * u)
    a, bv = _lru_gates(xc, wa_ref[0], ba_ref[...], wx_ref[0], bx_ref[...],
                       _softplus(-lam_ref[...]))
    hn = a * h_ref[...] + bv
    hn_ref[...] = hn
    y_ref[...] = (hn * gate_ref[...]).astype(y_ref.dtype)


def _lru_step(u, gate, conv_state2, h_state, conv_w, conv_b, w_a, b_a, w_x, b_x, lam):
    b, ch = u.shape
    nblk, bw, _ = w_a.shape
    kw = conv_w.shape[0]
    assert kw == 4
    col = lambda j: (0, j)
    vec = pl.BlockSpec((1, bw), col)
    state = [pl.BlockSpec((b, bw), (lambda j, kk=kk: (0, kk * nblk + j))) for kk in range(kw - 1)]
    return pl.pallas_call(
        _lru_step_kernel,
        grid=(nblk,),
        in_specs=[pl.BlockSpec((b, bw), col), pl.BlockSpec((b, bw), col), *state,
                  pl.BlockSpec((b, bw), col), pl.BlockSpec((kw, bw), col), vec,
                  pl.BlockSpec((1, bw, bw), lambda j: (j, 0, 0)), vec,
                  pl.BlockSpec((1, bw, bw), lambda j: (j, 0, 0)), vec, vec],
        out_specs=[pl.BlockSpec((b, bw), col), pl.BlockSpec((b, bw), col)],
        out_shape=[jax.ShapeDtypeStruct((b, ch), BF16), jax.ShapeDtypeStruct((b, ch), F32)],
        compiler_params=_cp("parallel"),
        name="lru_step",
    )(u, gate, conv_state2, conv_state2, conv_state2, h_state, conv_w, conv_b.reshape(1, ch),
      w_a, b_a.reshape(1, ch), w_x, b_x.reshape(1, ch), lam.reshape(1, ch))


def _ffn(x, gain, w_gate, w_up, w_down):
    (h,) = _matmul(x, [w_gate, w_up], name="ffn_up", gain=gain, epilogue=_ep_swiglu,
                   out_dtypes=[BF16], bm=1024, bn=512)
    (y,) = _matmul(h, [w_down], name="ffn_down", res=x, epilogue=_ep_res, out_dtypes=[F32],
                   bm=1024, bn=512)
    return y


def kernel(x_prompt, x_sample, cache_ckv, cache_krope, page_table, state_conv, state_h, norm_mix, norm_ffn, mla_w_dq, mla_q_norm, mla_w_uq, mla_w_dkv, mla_kv_norm, mla_w_uk, mla_w_uv, mla_q_gain, mla_k_gain, mla_w_o, lru_w_gate, lru_w_in, lru_conv_w, lru_conv_b, lru_w_a, lru_b_a, lru_w_x, lru_b_x, lru_lambda, lru_w_out, ffn_w_gate, ffn_w_up, ffn_w_down):
    bp, sp, d = x_prompt.shape
    bs, ss, _ = x_sample.shape
    assert ss == 1, "sample path handles one new token per sequence"
    depth = norm_mix.shape[0]
    n_heads, nope = mla_w_uk.shape[2], mla_w_uk.shape[3]
    q_lora, kv_lora = mla_w_dq.shape[2], mla_w_uk.shape[1]
    qk = mla_w_uq.shape[3]
    rope = qk - nope
    vd = mla_w_uv.shape[3]
    past = page_table.shape[1] * cache_ckv.shape[2]
    scale = float(qk) ** -0.5

    xp = x_prompt.reshape(bp * sp, d)
    xs = x_sample.reshape(bs, d)
    cos_p, sin_p = _rope_tables(jnp.arange(sp), rope)
    cos_s, sin_s = _rope_tables(jnp.full((bs,), past), rope)
    cache_krt = jnp.swapaxes(cache_krope, 2, 3)

    ckv_p, kr_p, ckv_s, kr_s = [], [], [], []
    conv_p, h_p, conv_s, h_s = [], [], [], []
    for i in range(depth):
        j = i // 2
        if i % 2 == 0:
            w_down = jnp.concatenate([mla_w_dq[j], mla_w_dkv[j]], axis=1).astype(BF16)
            w_uq = mla_w_uq[j]
            w_uq_h = jnp.transpose(w_uq, (1, 0, 2)).astype(BF16)
            w_uq_split = jnp.concatenate(
                [w_uq[:, :, :nope].reshape(q_lora, n_heads * nope),
                 w_uq[:, :, nope:].reshape(q_lora, n_heads * rope)], axis=1).astype(BF16)
            w_uk = mla_w_uk[j].reshape(kv_lora, n_heads * nope).astype(BF16)
            w_uv = mla_w_uv[j].reshape(kv_lora, n_heads * vd).astype(BF16)
            w_o = mla_w_o[j].reshape(n_heads * vd, d).astype(BF16)
            down = functools.partial(_qkv_down, gain=norm_mix[i], w_cat=w_down,
                                     q_norm=mla_q_norm[j], kv_norm=mla_kv_norm[j],
                                     q_lora=q_lora, kv_lora=kv_lora)
            qn, c1, cb1, r1 = down(xp, cos2=cos_p, sin2=sin_p)
            q = _q_up(qn, w_uq_split, jnp.tile(cos_p, (1, n_heads)), jnp.tile(sin_p, (1, n_heads)),
                      mla_q_gain[j], batch=bp, n_heads=n_heads, nope=nope, scale=scale * LOG2E)
            k, vt = _kv_up(cb1, r1, w_uk, w_uv.T, mla_k_gain[j], batch=bp, n_heads=n_heads,
                           nope=nope)
            o = _flash(q, k, vt).reshape(bp * sp, n_heads * vd)
            (xp,) = _matmul(o, [w_o], name="mla_out", res=xp, epilogue=_ep_res, out_dtypes=[F32],
                            bm=512, bn=d)
            qn, c2, _, r2 = down(xs, cos2=cos_s, sin2=sin_s)
            qlat, qrr, s_new = _dec_q(qn, w_uq_h, cos_s, sin_s, mla_q_gain[j], mla_k_gain[j],
                                      w_uk, c2, r2, nope=nope, scale=scale)
            o_lat = _dec_attn(page_table, qlat.reshape(bs, n_heads, kv_lora),
                              jnp.transpose(qrr, (1, 0, 2)),
                              s_new[:, :n_heads].reshape(bs, n_heads, 1),
                              c2.reshape(bs, 1, kv_lora), w_uk.T, cache_ckv, cache_krt,
                              layer=j, n_heads=n_heads, qk_dim=qk)
            o = _dec_ov(o_lat.reshape(bs, n_heads * kv_lora), w_uv, n_heads=n_heads)
            (xs,) = _matmul(o, [w_o], name="mla_out", res=xs, epilogue=_ep_res, out_dtypes=[F32],
                            bm=512, bn=d)
            ckv_p.append(c1.reshape(bp, sp, kv_lora)); kr_p.append(r1.reshape(bp, sp, rope))
            ckv_s.append(c2.reshape(bs, 1, kv_lora)); kr_s.append(r2.reshape(bs, 1, rope))
        else:
            w_gate = lru_w_gate[j].astype(BF16)
            w_in = lru_w_in[j].astype(BF16)
            w_out = lru_w_out[j].astype(BF16)
            w_a = lru_w_a[j].astype(BF16)
            w_x = lru_w_x[j].astype(BF16)
            rest = (lru_conv_w[j], lru_conv_b[j], w_a, lru_b_a[j].reshape(-1), w_x,
                    lru_b_x[j].reshape(-1), lru_lambda[j])
            gate, u = _matmul(xp, [w_gate, w_in], name="lru_in", gain=norm_mix[i],
                              epilogue=_ep_gelu_id, out_dtypes=[F32, F32], bm=1024, bn=512)
            y, cv1, hh1 = _lru_scan(u, gate, *rest, batch=bp)
            (xp,) = _matmul(y, [w_out], name="lru_out", res=xp, epilogue=_ep_res, out_dtypes=[F32],
                            bm=512, bn=d)
            gate, u = _matmul(xs, [w_gate, w_in], name="lru_in", gain=norm_mix[i],
                              epilogue=_ep_gelu_id, out_dtypes=[F32, F32], bm=1024, bn=512)
            y, hh2 = _lru_step(u, gate, state_conv[j].reshape(bs, -1), state_h[j], *rest)
            cv2 = jnp.concatenate([state_conv[j][:, 1:], u[:, None, :]], axis=1)
            (xs,) = _matmul(y, [w_out], name="lru_out", res=xs, epilogue=_ep_res, out_dtypes=[F32],
                            bm=512, bn=d)
            conv_p.append(cv1); h_p.append(hh1.reshape(bp, -1)); conv_s.append(cv2); h_s.append(hh2)
        wg, wu, wd = (ffn_w_gate[i].astype(BF16), ffn_w_up[i].astype(BF16),
                      ffn_w_down[i].astype(BF16))
        xp = _ffn(xp, norm_ffn[i], wg, wu, wd)
        xs = _ffn(xs, norm_ffn[i], wg, wu, wd)
    return (xp.reshape(bp, sp, d), xs.reshape(bs, ss, d),
            jnp.stack(ckv_p), jnp.stack(kr_p), jnp.stack(ckv_s), jnp.stack(kr_s),
            jnp.stack(conv_p), jnp.stack(h_p), jnp.stack(conv_s), jnp.stack(h_s))
```

```python
import functools

import jax
import jax.numpy as jnp
from jax import lax
from jax.experimental import pallas as pl
from jax.experimental.pallas import tpu as pltpu

F32 = jnp.float32
BF16 = jnp.bfloat16

EPS = 1e-6
NEG = -1e30
ROPE_BASE = 10000.0
LRU_C = 8.0
PAGE = 128
PAGES_PER_STEP = 64
FLASH_BQ = 1024
FLASH_HEADS = 4
LOG2E = 1.4426950408889634
VMEM_LIMIT = 56 * 1024 * 1024
NT = (((1,), (1,)), ((), ()))


def _cp(*sem):
    return pltpu.CompilerParams(dimension_semantics=sem, vmem_limit_bytes=VMEM_LIMIT)


def _blk(dim, pref):
    b = min(dim, pref)
    while dim % b:
        b //= 2
    return b


def _rms(xf, gain):
    ms = jnp.mean(xf * xf, axis=-1, keepdims=True)
    return (xf * lax.rsqrt(ms + EPS)) * gain


def _mm_kernel(*refs, n_w, has_norm, has_res, n_out, emit_bf16, epilogue):
    refs = list(refs)
    x_ref = refs.pop(0)
    g_ref = refs.pop(0) if has_norm else None
    w_refs = [refs.pop(0) for _ in range(n_w)]
    res_ref = refs.pop(0) if has_res else None
    o_refs = [refs.pop(0) for _ in range(n_out)]
    wb_refs = [refs.pop(0) for _ in range(n_w)] if emit_bf16 else []
    if has_norm:
        xn_ref = refs.pop(0)

        @pl.when(pl.program_id(1) == 0)
        def _():
            xn_ref[...] = _rms(x_ref[...], g_ref[...]).astype(BF16)

        xb = xn_ref[...]
    else:
        xb = x_ref[...]
    ws = [w[...].astype(BF16) for w in w_refs]
    for wb_ref, w in zip(wb_refs, ws):
        wb_ref[...] = w
    accs = [jnp.dot(xb, w, preferred_element_type=F32) for w in ws]
    outs = epilogue(accs, res_ref[...] if has_res else None)
    for o_ref, o in zip(o_refs, outs):
        o_ref[...] = o.astype(o_ref.dtype)


def _matmul(x, ws, *, name, epilogue, out_dtypes, gain=None, res=None, layer=None,
            bm=1024, bn=1024):
    m, k = x.shape
    n = ws[0].shape[-1]
    bm, bn = _blk(m, bm), _blk(n, bn)
    emit_bf16 = ws[0].dtype != BF16
    assert not emit_bf16 or m == bm
    w_spec = pl.BlockSpec((k, bn), lambda i, j: (0, j))
    w_in_spec = w_spec if layer is None else pl.BlockSpec((None, k, bn),
                                                          lambda i, j: (layer, 0, j))
    in_specs = [pl.BlockSpec((bm, k), lambda i, j: (i, 0))]
    args = [x]
    if gain is not None:
        in_specs.append(pl.BlockSpec((1, k), lambda i, j: (0, 0)))
        args.append(gain.reshape(1, k))
    for w in ws:
        in_specs.append(w_in_spec)
        args.append(w)
    if res is not None:
        in_specs.append(pl.BlockSpec((bm, bn), lambda i, j: (i, j)))
        args.append(res)
    kern = functools.partial(_mm_kernel, n_w=len(ws), has_norm=gain is not None,
                             has_res=res is not None, n_out=len(out_dtypes),
                             emit_bf16=emit_bf16, epilogue=epilogue)
    n_wb = len(ws) if emit_bf16 else 0
    outs = pl.pallas_call(
        kern,
        grid=(m // bm, n // bn),
        in_specs=in_specs,
        out_specs=([pl.BlockSpec((bm, bn), lambda i, j: (i, j)) for _ in out_dtypes]
                   + [w_spec] * n_wb),
        out_shape=([jax.ShapeDtypeStruct((m, n), dt) for dt in out_dtypes]
                   + [jax.ShapeDtypeStruct((k, n), BF16)] * n_wb),
        scratch_shapes=[pltpu.VMEM((bm, k), BF16)] if gain is not None else [],
        compiler_params=_cp("parallel", "arbitrary"),
        name=name,
    )(*args)
    return outs


def _ep_res(accs, res):
    return [res + accs[0]]


def _ep_swiglu(accs, res):
    g, u = accs
    return [(g * jax.nn.sigmoid(g)) * u]


def _ep_gelu_id(accs, res):
    return [jax.nn.gelu(accs[0]), accs[1]]


def _rope_tables(pos, rope_dim):
    half = rope_dim // 2
    inv = ROPE_BASE ** (-jnp.arange(half, dtype=F32) / half)
    ang = pos.astype(F32)[:, None] * inv[None, :]
    cos, sin = jnp.cos(ang), jnp.sin(ang)
    return jnp.concatenate([cos, cos], -1), jnp.concatenate([-sin, sin], -1)


def _rope(x, cos2, sin2):
    half = x.shape[-1] // 2
    swapped = jnp.concatenate([x[:, half:], x[:, :half]], axis=-1)
    return x * cos2 + swapped * sin2


def _swap_halves(x, half):
    lane = lax.broadcasted_iota(jnp.int32, x.shape, 1)
    n = x.shape[1]
    return jnp.where(lane % (2 * half) < half, pltpu.roll(x, n - half, 1), pltpu.roll(x, half, 1))


def _qkv_down_kernel(x_ref, g_ref, w_ref, qn_g_ref, kvn_g_ref, cos_ref, sin_ref,
                     qn_ref, c_ref, cb_ref, kr_ref, *, q_lora, kv_lora):
    hb = _rms(x_ref[...], g_ref[...]).astype(BF16)
    a = jnp.dot(hb, w_ref[...], preferred_element_type=F32)
    qn_ref[...] = _rms(a[:, :q_lora], qn_g_ref[...]).astype(BF16)
    c = _rms(a[:, q_lora:q_lora + kv_lora], kvn_g_ref[...])
    c_ref[...] = c
    cb_ref[...] = c.astype(BF16)
    kr_ref[...] = _rope(a[:, q_lora + kv_lora:], cos_ref[...], sin_ref[...])


def _qkv_down(x, gain, w_cat, q_norm, kv_norm, cos2, sin2, *, q_lora, kv_lora):
    m, d = x.shape
    n = w_cat.shape[1]
    rope = n - q_lora - kv_lora
    bm = _blk(m, 512)
    nt = cos2.shape[0] // bm
    row = lambda i: (i, 0)
    fixed = lambda i: (0, 0)
    tab = lambda i: (i % nt, 0)
    return pl.pallas_call(
        functools.partial(_qkv_down_kernel, q_lora=q_lora, kv_lora=kv_lora),
        grid=(m // bm,),
        in_specs=[pl.BlockSpec((bm, d), row), pl.BlockSpec((1, d), fixed),
                  pl.BlockSpec((d, n), fixed), pl.BlockSpec((1, q_lora), fixed),
                  pl.BlockSpec((1, kv_lora), fixed),
                  pl.BlockSpec((bm, rope), tab), pl.BlockSpec((bm, rope), tab)],
        out_specs=[pl.BlockSpec((bm, q_lora), row), pl.BlockSpec((bm, kv_lora), row),
                   pl.BlockSpec((bm, kv_lora), row), pl.BlockSpec((bm, rope), row)],
        out_shape=[jax.ShapeDtypeStruct((m, q_lora), BF16),
                   jax.ShapeDtypeStruct((m, kv_lora), F32),
                   jax.ShapeDtypeStruct((m, kv_lora), BF16),
                   jax.ShapeDtypeStruct((m, rope), F32)],
        compiler_params=_cp("parallel"),
        name="qkv_down",
    )(x, gain.reshape(1, d), w_cat, q_norm.reshape(1, -1), kv_norm.reshape(1, -1), cos2, sin2)


def _q_up_kernel(qn_ref, w_ref, cos_ref, sin_ref, g_ref, e_ref, et_ref, q_ref, *, heads, nope):
    a = jnp.dot(qn_ref[...], w_ref[...], preferred_element_type=F32)
    hn = heads * nope
    rope = (a.shape[1] - hn) // heads
    qk = nope + rope
    r = a[:, hn:]
    r = r * cos_ref[...] + _swap_halves(r, rope // 2) * sin_ref[...]
    x = jnp.concatenate([a[:, :hn], r], axis=1)
    ss = jnp.dot((x * x).astype(BF16), e_ref[...], preferred_element_type=F32)
    inv = lax.rsqrt(ss / qk + EPS)
    hi = inv.astype(BF16)
    lo = (inv - hi.astype(F32)).astype(BF16)
    invb = jnp.dot(jnp.concatenate([hi, lo], axis=1), et_ref[...], preferred_element_type=F32)
    y = ((x * invb) * g_ref[...]).astype(BF16)
    for i in range(heads):
        q_ref[0, i, :, :nope] = y[:, i * nope:(i + 1) * nope]
        q_ref[0, i, :, nope:] = y[:, hn + i * rope:hn + (i + 1) * rope]


def _q_up(qn, w_uq_split, cos_t, sin_t, q_gain, *, batch, n_heads, nope, scale):
    m, r = qn.shape
    n = w_uq_split.shape[1]
    qk = n // n_heads
    rope = qk - nope
    s = m // batch
    bm = _blk(s, 512)
    ns = s // bm
    lanes = 128
    assert n_heads <= lanes
    head_of_col = jnp.concatenate([jnp.arange(n_heads * nope) // nope,
                                   jnp.arange(n_heads * rope) // rope])
    e = (head_of_col[:, None] == jnp.arange(lanes)[None, :]).astype(BF16)
    et2 = jnp.concatenate([e.T, e.T], axis=0)
    g_cols = jnp.concatenate([jnp.tile(q_gain[:nope], n_heads), jnp.tile(q_gain[nope:], n_heads)])
    fixed = lambda i: (0, 0)
    tab = lambda i: (i % ns, 0)
    return pl.pallas_call(
        functools.partial(_q_up_kernel, heads=n_heads, nope=nope),
        grid=(m // bm,),
        in_specs=[pl.BlockSpec((bm, r), lambda i: (i, 0)), pl.BlockSpec((r, n), fixed),
                  pl.BlockSpec((bm, n_heads * rope), tab), pl.BlockSpec((bm, n_heads * rope), tab),
                  pl.BlockSpec((1, n), fixed), pl.BlockSpec((n, lanes), fixed),
                  pl.BlockSpec((2 * lanes, n), fixed)],
        out_specs=pl.BlockSpec((1, n_heads, bm, qk), lambda i: (i // ns, 0, i % ns, 0)),
        out_shape=jax.ShapeDtypeStruct((batch, n_heads, s, qk), BF16),
        compiler_params=_cp("parallel"),
        name="q_up",
    )(qn, w_uq_split, cos_t, sin_t, (g_cols * scale).reshape(1, n), e, et2)


def _kv_up_kernel(cb_ref, kr_ref, wuk_ref, wuvt_ref, g_ref, k_ref, vt_ref, *, heads, nope):
    cb = cb_ref[...]
    kr = kr_ref[...]
    qk = nope + kr.shape[-1]
    kn = jnp.dot(cb, wuk_ref[...], preferred_element_type=F32)
    vt = lax.dot_general(wuvt_ref[...], cb, NT, preferred_element_type=F32)
    ssr = jnp.sum(kr * kr, axis=-1, keepdims=True)
    g = g_ref[...]
    vd = vt.shape[0] // heads
    for i in range(heads):
        kh = kn[:, i * nope:(i + 1) * nope]
        inv = lax.rsqrt((jnp.sum(kh * kh, axis=-1, keepdims=True) + ssr) / qk + EPS)
        k_ref[0, i, :, :nope] = ((kh * inv) * g[:, :nope]).astype(BF16)
        k_ref[0, i, :, nope:] = ((kr * inv) * g[:, nope:]).astype(BF16)
        vt_ref[0, i] = vt[i * vd:(i + 1) * vd].astype(BF16)


def _kv_up(cb, kr, w_uk, w_uvt, k_gain, *, batch, n_heads, nope):
    m, c = cb.shape
    rope = kr.shape[1]
    qk = nope + rope
    vd = w_uvt.shape[0] // n_heads
    s = m // batch
    bm = _blk(s, 512)
    ns = s // bm
    return pl.pallas_call(
        functools.partial(_kv_up_kernel, heads=n_heads, nope=nope),
        grid=(m // bm,),
        in_specs=[pl.BlockSpec((bm, c), lambda i: (i, 0)),
                  pl.BlockSpec((bm, rope), lambda i: (i, 0)),
                  pl.BlockSpec(w_uk.shape, lambda i: (0, 0)),
                  pl.BlockSpec(w_uvt.shape, lambda i: (0, 0)),
                  pl.BlockSpec((1, qk), lambda i: (0, 0))],
        out_specs=[pl.BlockSpec((1, n_heads, bm, qk), lambda i: (i // ns, 0, i % ns, 0)),
                   pl.BlockSpec((1, n_heads, vd, bm), lambda i: (i // ns, 0, 0, i % ns))],
        out_shape=[jax.ShapeDtypeStruct((batch, n_heads, s, qk), BF16),
                   jax.ShapeDtypeStruct((batch, n_heads, vd, s), BF16)],
        compiler_params=_cp("parallel"),
        name="kv_up",
    )(cb, kr, w_uk, w_uvt, k_gain.reshape(1, qk))


def _flash_kernel(q_ref, k_ref, vt_ref, o_ref, *, bq):
    s_len = q_ref.shape[2]
    vd = vt_ref.shape[2]
    kpos = lax.broadcasted_iota(jnp.int32, (bq, bq), 0)
    qpos = lax.broadcasted_iota(jnp.int32, (bq, bq), 1)
    for hh in range(q_ref.shape[1]):
        for qi in range(s_len // bq):
            nk = (qi + 1) * bq
            q = q_ref[0, hh, qi * bq:nk, :]
            st = lax.dot_general(k_ref[0, hh, 0:nk, :], q, NT,
                                 preferred_element_type=F32)
            diag = jnp.where(kpos <= qpos, st[nk - bq:], NEG)
            st = diag if qi == 0 else jnp.concatenate([st[:nk - bq], diag], axis=0)
            m = jnp.max(st, axis=0, keepdims=True)
            p = jnp.exp2(st - m)
            l = jnp.sum(p, axis=0, keepdims=True)
            acct = jnp.dot(vt_ref[0, hh, :, 0:nk], p.astype(BF16),
                           preferred_element_type=F32)
            o_ref[0, qi * bq:nk, hh * vd:(hh + 1) * vd] = jnp.transpose(acct / l).astype(
                o_ref.dtype)


def _flash(q, k, vt):
    b, h, s, qk = q.shape
    vd = vt.shape[2]
    bq = _blk(s, FLASH_BQ)
    hb = _blk(h, FLASH_HEADS)
    head = lambda bi, hi: (bi, hi, 0, 0)
    return pl.pallas_call(
        functools.partial(_flash_kernel, bq=bq),
        grid=(b, h // hb),
        in_specs=[pl.BlockSpec((1, hb, s, qk), head), pl.BlockSpec((1, hb, s, qk), head),
                  pl.BlockSpec((1, hb, vd, s), head)],
        out_specs=pl.BlockSpec((1, s, hb * vd), lambda bi, hi: (bi, 0, hi)),
        out_shape=jax.ShapeDtypeStruct((b, s, h * vd), BF16),
        compiler_params=_cp("parallel", "parallel"),
        name="flash",
    )(q, k, vt)


def _dec_q_kernel(qn_ref, wq_ref, cos_ref, sin_ref, qg_ref, kg_ref, wuk_ref, c_ref, kr_ref,
                  qlat_ref, qrr_ref, s_ref, *, nope, scale):
    hd = pl.program_id(0)
    q = jnp.dot(qn_ref[...], wq_ref[0], preferred_element_type=F32)
    q = jnp.concatenate([q[:, :nope], _rope(q[:, nope:], cos_ref[...], sin_ref[...])], axis=-1)
    qg = (_rms(q, qg_ref[...]) * kg_ref[...]) * scale
    wuk = wuk_ref[...]
    qlat = lax.dot_general(qg[:, :nope].astype(BF16), wuk, NT, preferred_element_type=F32)
    qrr = qg[:, nope:]
    qlat_ref[...] = qlat.astype(BF16)
    qrr_ref[0] = qrr.astype(BF16)
    c = c_ref[...]
    kr = kr_ref[...]
    kn = jnp.dot(c.astype(BF16), wuk, preferred_element_type=F32)
    ss = jnp.sum(kn * kn, axis=-1, keepdims=True) + jnp.sum(kr * kr, axis=-1, keepdims=True)
    inv = lax.rsqrt(ss / (nope + kr.shape[-1]) + EPS)
    raw = (jnp.sum(qlat * c, axis=-1, keepdims=True)
           + jnp.sum(qrr * kr, axis=-1, keepdims=True))
    lane = lax.broadcasted_iota(jnp.int32, s_ref.shape, 1)

    @pl.when(hd == 0)
    def _():
        s_ref[...] = jnp.zeros_like(s_ref)

    s_ref[...] = jnp.where(lane == hd, raw * inv, s_ref[...])


def _dec_q(qn, w_uq_h, cos2, sin2, q_gain, k_gain, w_uk, c_new, kr_new, *, nope, scale):
    b, r = qn.shape
    h, _, qk = w_uq_h.shape
    c = c_new.shape[1]
    rope = qk - nope
    fixed = lambda j: (0, 0)
    return pl.pallas_call(
        functools.partial(_dec_q_kernel, nope=nope, scale=scale),
        grid=(h,),
        in_specs=[pl.BlockSpec((b, r), fixed), pl.BlockSpec((1, r, qk), lambda j: (j, 0, 0)),
                  pl.BlockSpec((b, rope), fixed), pl.BlockSpec((b, rope), fixed),
                  pl.BlockSpec((1, qk), fixed), pl.BlockSpec((1, qk), fixed),
                  pl.BlockSpec((c, nope), lambda j: (0, j)),
                  pl.BlockSpec((b, c), fixed), pl.BlockSpec((b, rope), fixed)],
        out_specs=[pl.BlockSpec((b, c), lambda j: (0, j)),
                   pl.BlockSpec((1, b, rope), lambda j: (j, 0, 0)),
                   pl.BlockSpec((b, 128), fixed)],
        out_shape=[jax.ShapeDtypeStruct((b, h * c), BF16),
                   jax.ShapeDtypeStruct((h, b, rope), BF16),
                   jax.ShapeDtypeStruct((b, 128), F32)],
        compiler_params=_cp("arbitrary"),
        name="dec_q",
    )(qn, w_uq_h, cos2, sin2, q_gain.reshape(1, qk), k_gain.reshape(1, qk), w_uk, c_new, kr_new)


def _dec_attn_kernel(pt_ref, qlat_ref, qrr_ref, snew_ref, cnew_ref, wukt_ref, ckv_hbm, krt_hbm,
                     o_ref, cbuf, krbuf, lhs_ref, sem, m_ref, l_ref, acc_ref, *,
                     layer, n_heads, qk_dim):
    b = pl.program_id(0)
    ch = pl.program_id(1)
    nb = pl.num_programs(0)
    nch = pl.num_programs(1)
    step = b * nch + ch
    slot = lax.rem(step, 2)
    n_rows = wukt_ref.shape[0]

    def copies(bb, cc, sl):
        out = []
        for i in range(PAGES_PER_STEP):
            pg = pt_ref[bb, cc * PAGES_PER_STEP + i]
            toks = pl.ds(i * PAGE, PAGE)
            out.append(pltpu.make_async_copy(ckv_hbm.at[layer, pg], cbuf.at[sl, toks],
                                             sem.at[0, sl]))
            out.append(pltpu.make_async_copy(krt_hbm.at[layer, pg], krbuf.at[sl, :, toks],
                                             sem.at[1, sl]))
        return out

    @pl.when(step == 0)
    def _():
        lhs_ref[0:n_rows] = wukt_ref[...]
        for cp in copies(0, 0, 0):
            cp.start()

    @pl.when(step + 1 < nb * nch)
    def _():
        wrap = ch + 1 == nch
        for cp in copies(jnp.where(wrap, b + 1, b), jnp.where(wrap, 0, ch + 1), 1 - slot):
            cp.start()

    @pl.when(ch == 0)
    def _():
        lhs_ref[n_rows:] = qlat_ref[0]
        m_ref[...] = snew_ref[0]
        l_ref[...] = jnp.ones_like(l_ref)
        acc_ref[...] = jnp.broadcast_to(cnew_ref[0], acc_ref.shape)

    pltpu.make_async_copy(cbuf.at[slot], cbuf.at[slot], sem.at[0, slot]).wait()
    pltpu.make_async_copy(krbuf.at[slot], krbuf.at[slot], sem.at[1, slot]).wait()

    nope = n_rows // n_heads
    t = cbuf.shape[1]
    cb = cbuf[slot].astype(BF16)
    krt = krbuf[slot]
    kq = lax.dot_general(lhs_ref[...], cb, NT, preferred_element_type=F32)
    knt = kq[:n_rows]
    ss = jnp.sum((knt * knt).reshape(n_heads, nope, t), axis=1)
    ssr = jnp.sum(krt * krt, axis=0, keepdims=True)
    raw = kq[n_rows:] + jnp.dot(qrr_ref[0], krt.astype(BF16), preferred_element_type=F32)
    s = raw * lax.rsqrt((ss + ssr) / qk_dim + EPS)
    m_prev = m_ref[...]
    m_new = jnp.maximum(m_prev, jnp.max(s, axis=1, keepdims=True))
    corr = jnp.exp(m_prev - m_new)
    p = jnp.exp(s - m_new)
    l_ref[...] = l_ref[...] * corr + jnp.sum(p, axis=1, keepdims=True)
    acc_ref[...] = acc_ref[...] * corr + jnp.dot(p.astype(BF16), cb, preferred_element_type=F32)
    m_ref[...] = m_new

    @pl.when(ch == nch - 1)
    def _():
        o_ref[0] = acc_ref[...] / l_ref[...]


def _dec_attn(page_table, qlat, qrr, s_new, c_new, w_ukt, cache_ckv, cache_krt, *,
              layer, n_heads, qk_dim):
    b, n_pages = page_table.shape
    c = c_new.shape[-1]
    rope = cache_krt.shape[2]
    t = PAGES_PER_STEP * PAGE
    assert n_pages % PAGES_PER_STEP == 0
    assert cache_ckv.shape[2] == PAGE and cache_krt.shape[3] == PAGE
    per_b = lambda bi, ci, pt: (bi, 0, 0)
    grid_spec = pltpu.PrefetchScalarGridSpec(
        num_scalar_prefetch=1,
        grid=(b, n_pages // PAGES_PER_STEP),
        in_specs=[pl.BlockSpec((1, n_heads, c), per_b),
                  pl.BlockSpec((1, n_heads, rope), per_b),
                  pl.BlockSpec((1, n_heads, 1), per_b),
                  pl.BlockSpec((1, 1, c), per_b),
                  pl.BlockSpec(w_ukt.shape, lambda bi, ci, pt: (0, 0)),
                  pl.BlockSpec(memory_space=pl.ANY),
                  pl.BlockSpec(memory_space=pl.ANY)],
        out_specs=pl.BlockSpec((1, n_heads, c), per_b),
        scratch_shapes=[pltpu.VMEM((2, t, c), F32), pltpu.VMEM((2, rope, t), F32),
                        pltpu.VMEM((w_ukt.shape[0] + n_heads, c), BF16),
                        pltpu.SemaphoreType.DMA((2, 2)),
                        pltpu.VMEM((n_heads, 1), F32), pltpu.VMEM((n_heads, 1), F32),
                        pltpu.VMEM((n_heads, c), F32)])
    return pl.pallas_call(
        functools.partial(_dec_attn_kernel, layer=layer, n_heads=n_heads, qk_dim=qk_dim),
        grid_spec=grid_spec,
        out_shape=jax.ShapeDtypeStruct((b, n_heads, c), F32),
        compiler_params=_cp("arbitrary", "arbitrary"),
        name="dec_attn",
    )(page_table, qlat, qrr, s_new, c_new, w_ukt, cache_ckv, cache_krt)


def _dec_ov_kernel(o_ref, w_ref, out_ref):
    out_ref[...] = jnp.dot(o_ref[...].astype(BF16), w_ref[...],
                           preferred_element_type=F32).astype(out_ref.dtype)


def _dec_ov(o_lat2, w_uv, *, n_heads):
    b, hc = o_lat2.shape
    c = hc // n_heads
    vd = w_uv.shape[1] // n_heads
    return pl.pallas_call(
        _dec_ov_kernel,
        grid=(n_heads,),
        in_specs=[pl.BlockSpec((b, c), lambda j: (0, j)), pl.BlockSpec((c, vd), lambda j: (0, j))],
        out_specs=pl.BlockSpec((b, vd), lambda j: (0, j)),
        out_shape=jax.ShapeDtypeStruct((b, n_heads * vd), BF16),
        compiler_params=_cp("parallel"),
        name="dec_ov",
    )(o_lat2, w_uv)


def _lru_gates(xc, wa, ba, wx, bx, sp):
    xb = xc.astype(BF16)
    r = jax.nn.sigmoid(jnp.dot(xb, wa, preferred_element_type=F32) + ba)
    ig = jax.nn.sigmoid(jnp.dot(xb, wx, preferred_element_type=F32) + bx)
    log_a = (-LRU_C * r) * sp
    a = jnp.exp(log_a)
    bmul = jnp.sqrt(jnp.tanh(-log_a) * (a * a + 1.0))
    return a, bmul * (ig * xc)


def _softplus(x):
    return jnp.maximum(x, 0.0) + jnp.log1p(jnp.exp(-jnp.abs(x)))


def _lru_scan_kernel(u_ref, gate_ref, cw_ref, cb_ref, wa_ref, ba_ref, wx_ref, bx_ref, lam_ref,
                     y_ref, conv_ref, hlast_ref, halo_ref, hc_ref, *, bw):
    ti = pl.program_id(1)
    tb, ch = u_ref.shape

    @pl.when(ti == 0)
    def _():
        halo_ref[...] = jnp.zeros_like(halo_ref)
        hc_ref[...] = jnp.zeros_like(hc_ref)

    u = u_ref[...]
    ext = jnp.concatenate([halo_ref[...], u], axis=0)
    cw = cw_ref[...]
    xc = cb_ref[...] + (((cw[0:1] * ext[5:5 + tb] + cw[1:2] * ext[6:6 + tb])
                         + cw[2:3] * ext[7:7 + tb]) + cw[3:4] * u)
    halo_ref[...] = u[tb - 8:]
    sp = _softplus(-lam_ref[...])
    row = lax.broadcasted_iota(jnp.int32, (tb // 8, 8, bw), 1)
    for n in range(ch // bw):
        cs = slice(n * bw, (n + 1) * bw)
        a, bv = _lru_gates(xc[:, cs], wa_ref[n], ba_ref[:, cs], wx_ref[n], bx_ref[:, cs],
                           sp[:, cs])
        a = a.reshape(tb // 8, 8, bw)
        bv = bv.reshape(tb // 8, 8, bw)
        for sh in (1, 2, 4):
            keep = row >= sh
            bv = jnp.where(keep, a * pltpu.roll(bv, sh, 1) + bv, bv)
            a = jnp.where(keep, a * pltpu.roll(a, sh, 1), a)
        a = a.reshape(tb, bw)
        bv = bv.reshape(tb, bw)
        hprev = hc_ref[:, cs]
        hs = []
        for g in range(tb // 8):
            hg = a[g * 8:(g + 1) * 8] * hprev + bv[g * 8:(g + 1) * 8]
            hs.append(hg)
            hprev = hg[7:8]
        hc_ref[:, cs] = hprev
        y_ref[:, cs] = (jnp.concatenate(hs, axis=0)
                        * gate_ref[:, cs].astype(F32)).astype(y_ref.dtype)

    @pl.when(ti == pl.num_programs(1) - 1)
    def _():
        conv_ref[0] = u[tb - 3:]
        hlast_ref[0] = hc_ref[...]


def _lru_scan(u, gate, conv_w, conv_b, w_a, b_a, w_x, b_x, lam, *, batch):
    m, ch = u.shape
    t = m // batch
    tb = _blk(t, 256)
    nt = t // tb
    nblk, bw, _ = w_a.shape
    kw = conv_w.shape[0]
    assert kw == 4 and tb % 8 == 0
    row = lambda bi, ti: (bi * nt + ti, 0)
    fixed2 = lambda bi, ti: (0, 0)
    fixed3 = lambda bi, ti: (0, 0, 0)
    per_b = lambda bi, ti: (bi, 0, 0)
    return pl.pallas_call(
        functools.partial(_lru_scan_kernel, bw=bw),
        grid=(batch, nt),
        in_specs=[pl.BlockSpec((tb, ch), row), pl.BlockSpec((tb, ch), row),
                  pl.BlockSpec((kw, ch), fixed2), pl.BlockSpec((1, ch), fixed2),
                  pl.BlockSpec((nblk, bw, bw), fixed3), pl.BlockSpec((1, ch), fixed2),
                  pl.BlockSpec((nblk, bw, bw), fixed3), pl.BlockSpec((1, ch), fixed2),
                  pl.BlockSpec((1, ch), fixed2)],
        out_specs=[pl.BlockSpec((tb, ch), row), pl.BlockSpec((1, kw - 1, ch), per_b),
                   pl.BlockSpec((1, 1, ch), per_b)],
        out_shape=[jax.ShapeDtypeStruct((m, ch), BF16),
                   jax.ShapeDtypeStruct((batch, kw - 1, ch), F32),
                   jax.ShapeDtypeStruct((batch, 1, ch), F32)],
        scratch_shapes=[pltpu.VMEM((8, ch), F32), pltpu.VMEM((1, ch), F32)],
        compiler_params=_cp("parallel", "arbitrary"),
        name="lru_scan",
    )(u, gate, conv_w, conv_b.reshape(1, ch), w_a, b_a.reshape(1, ch), w_x, b_x.reshape(1, ch),
      lam.reshape(1, ch))


def _lru_step_kernel(u_ref, gate_ref, s0_ref, s1_ref, s2_ref, h_ref, cw_ref, cb_ref, wa_ref,
                     ba_ref, wx_ref, bx_ref, lam_ref, y_ref, hn_ref):
    cw = cw_ref[...]
    u = u_ref[...]
    xc = cb_ref[...] + (((cw[0:1] * s0_ref[...] + cw[1:2] * s1_ref[...])
                         + cw[2:3] * s2_ref[...]) + cw[3:4] * u)
    a, bv = _lru_gates(xc, wa_ref[0], ba_ref[...], wx_ref[0], bx_ref[...],
                       _softplus(-lam_ref[...]))
    hn = a * h_ref[...] + bv
    hn_ref[...] = hn
    y_ref[...] = (hn * gate_ref[...].astype(F32)).astype(y_ref.dtype)


def _lru_step(u, gate, conv_state2, h_state, conv_w, conv_b, w_a, b_a, w_x, b_x, lam):
    b, ch = u.shape
    nblk, bw, _ = w_a.shape
    kw = conv_w.shape[0]
    assert kw == 4
    col = lambda j: (0, j)
    vec = pl.BlockSpec((1, bw), col)
    state = [pl.BlockSpec((b, bw), (lambda j, kk=kk: (0, kk * nblk + j))) for kk in range(kw - 1)]
    return pl.pallas_call(
        _lru_step_kernel,
        grid=(nblk,),
        in_specs=[pl.BlockSpec((b, bw), col), pl.BlockSpec((b, bw), col), *state,
                  pl.BlockSpec((b, bw), col), pl.BlockSpec((kw, bw), col), vec,
                  pl.BlockSpec((1, bw, bw), lambda j: (j, 0, 0)), vec,
                  pl.BlockSpec((1, bw, bw), lambda j: (j, 0, 0)), vec, vec],
        out_specs=[pl.BlockSpec((b, bw), col), pl.BlockSpec((b, bw), col)],
        out_shape=[jax.ShapeDtypeStruct((b, ch), BF16), jax.ShapeDtypeStruct((b, ch), F32)],
        compiler_params=_cp("parallel"),
        name="lru_step",
    )(u, gate, conv_state2, conv_state2, conv_state2, h_state, conv_w, conv_b.reshape(1, ch),
      w_a, b_a.reshape(1, ch), w_x, b_x.reshape(1, ch), lam.reshape(1, ch))


def _ffn(x, gain, w_gate, w_up, w_down, layer=None):
    h, *wb_up = _matmul(x, [w_gate, w_up], name="ffn_up", gain=gain, epilogue=_ep_swiglu,
                        out_dtypes=[BF16], layer=layer, bm=1024, bn=512)
    y, *wb_down = _matmul(h, [w_down], name="ffn_down", res=x, epilogue=_ep_res,
                          out_dtypes=[F32], layer=layer, bm=1024, bn=512)
    return y, (*wb_up, *wb_down)


def kernel(x_prompt, x_sample, cache_ckv, cache_krope, page_table, state_conv, state_h, norm_mix, norm_ffn, mla_w_dq, mla_q_norm, mla_w_uq, mla_w_dkv, mla_kv_norm, mla_w_uk, mla_w_uv, mla_q_gain, mla_k_gain, mla_w_o, lru_w_gate, lru_w_in, lru_conv_w, lru_conv_b, lru_w_a, lru_b_a, lru_w_x, lru_b_x, lru_lambda, lru_w_out, ffn_w_gate, ffn_w_up, ffn_w_down):
    bp, sp, d = x_prompt.shape
    bs, ss, _ = x_sample.shape
    assert ss == 1, "sample path handles one new token per sequence"
    depth = norm_mix.shape[0]
    n_heads, nope = mla_w_uk.shape[2], mla_w_uk.shape[3]
    q_lora, kv_lora = mla_w_dq.shape[2], mla_w_uk.shape[1]
    qk = mla_w_uq.shape[3]
    rope = qk - nope
    vd = mla_w_uv.shape[3]
    past = page_table.shape[1] * cache_ckv.shape[2]
    scale = float(qk) ** -0.5

    xp = x_prompt.reshape(bp * sp, d)
    xs = x_sample.reshape(bs, d)
    cos_p, sin_p = _rope_tables(jnp.arange(sp), rope)
    cos_s, sin_s = _rope_tables(jnp.full((bs,), past), rope)
    cache_krt = jnp.swapaxes(cache_krope, 2, 3)

    ckv_p, kr_p, ckv_s, kr_s = [], [], [], []
    conv_p, h_p, conv_s, h_s = [], [], [], []
    for i in range(depth):
        j = i // 2
        if i % 2 == 0:
            w_down = jnp.concatenate([mla_w_dq[j], mla_w_dkv[j]], axis=1).astype(BF16)
            w_uq = mla_w_uq[j]
            w_uq_h = jnp.transpose(w_uq, (1, 0, 2)).astype(BF16)
            w_uq_split = jnp.concatenate(
                [w_uq[:, :, :nope].reshape(q_lora, n_heads * nope),
                 w_uq[:, :, nope:].reshape(q_lora, n_heads * rope)], axis=1).astype(BF16)
            w_uk = mla_w_uk[j].reshape(kv_lora, n_heads * nope).astype(BF16)
            w_uv = mla_w_uv[j].reshape(kv_lora, n_heads * vd).astype(BF16)
            down = functools.partial(_qkv_down, gain=norm_mix[i], w_cat=w_down,
                                     q_norm=mla_q_norm[j], kv_norm=mla_kv_norm[j],
                                     q_lora=q_lora, kv_lora=kv_lora)
            qn, c2, _, r2 = down(xs, cos2=cos_s, sin2=sin_s)
            qlat, qrr, s_new = _dec_q(qn, w_uq_h, cos_s, sin_s, mla_q_gain[j], mla_k_gain[j],
                                      w_uk, c2, r2, nope=nope, scale=scale)
            o_lat = _dec_attn(page_table, qlat.reshape(bs, n_heads, kv_lora),
                              jnp.transpose(qrr, (1, 0, 2)),
                              s_new[:, :n_heads].reshape(bs, n_heads, 1),
                              c2.reshape(bs, 1, kv_lora), w_uk.T, cache_ckv, cache_krt,
                              layer=j, n_heads=n_heads, qk_dim=qk)
            o = _dec_ov(o_lat.reshape(bs, n_heads * kv_lora), w_uv, n_heads=n_heads)
            xs, w_o = _matmul(o, [mla_w_o.reshape(-1, n_heads * vd, d)], name="mla_out", res=xs,
                              epilogue=_ep_res, out_dtypes=[F32], layer=j, bn=512)
            qn, c1, cb1, r1 = down(xp, cos2=cos_p, sin2=sin_p)
            q = _q_up(qn, w_uq_split, jnp.tile(cos_p, (1, n_heads)), jnp.tile(sin_p, (1, n_heads)),
                      mla_q_gain[j], batch=bp, n_heads=n_heads, nope=nope, scale=scale * LOG2E)
            k, vt = _kv_up(cb1, r1, w_uk, w_uv.T, mla_k_gain[j], batch=bp, n_heads=n_heads,
                           nope=nope)
            o = _flash(q, k, vt).reshape(bp * sp, n_heads * vd)
            (xp,) = _matmul(o, [w_o], name="mla_out", res=xp, epilogue=_ep_res, out_dtypes=[F32],
                            bm=512, bn=d)
            ckv_p.append(c1.reshape(bp, sp, kv_lora)); kr_p.append(r1.reshape(bp, sp, rope))
            ckv_s.append(c2.reshape(bs, 1, kv_lora)); kr_s.append(r2.reshape(bs, 1, rope))
        else:
            w_a = lru_w_a[j].astype(BF16)
            w_x = lru_w_x[j].astype(BF16)
            rest = (lru_conv_w[j], lru_conv_b[j], w_a, lru_b_a[j].reshape(-1), w_x,
                    lru_b_x[j].reshape(-1), lru_lambda[j])
            gate, u, w_gate, w_in = _matmul(xs, [lru_w_gate, lru_w_in], name="lru_in",
                                            gain=norm_mix[i], epilogue=_ep_gelu_id,
                                            out_dtypes=[BF16, F32], layer=j, bn=512)
            y, hh2 = _lru_step(u, gate, state_conv[j].reshape(bs, -1), state_h[j], *rest)
            cv2 = jnp.concatenate([state_conv[j][:, 1:], u[:, None, :]], axis=1)
            xs, w_out = _matmul(y, [lru_w_out], name="lru_out", res=xs, epilogue=_ep_res,
                                out_dtypes=[F32], layer=j, bn=512)
            gate, u = _matmul(xp, [w_gate, w_in], name="lru_in", gain=norm_mix[i],
                              epilogue=_ep_gelu_id, out_dtypes=[BF16, F32], bm=1024, bn=512)
            y, cv1, hh1 = _lru_scan(u, gate, *rest, batch=bp)
            (xp,) = _matmul(y, [w_out], name="lru_out", res=xp, epilogue=_ep_res, out_dtypes=[F32],
                            bm=512, bn=d)
            conv_p.append(cv1); h_p.append(hh1.reshape(bp, -1)); conv_s.append(cv2); h_s.append(hh2)
        xs, w_ffn = _ffn(xs, norm_ffn[i], ffn_w_gate, ffn_w_up, ffn_w_down, layer=i)
        xp, _ = _ffn(xp, norm_ffn[i], *w_ffn)
    return (xp.reshape(bp, sp, d), xs.reshape(bs, ss, d),
            jnp.stack(ckv_p), jnp.stack(kr_p), jnp.stack(ckv_s), jnp.stack(kr_s),
            jnp.stack(conv_p), jnp.stack(h_p), jnp.stack(conv_s), jnp.stack(h_s))
```
